```python
import math
import jax, jax.numpy as jnp
from jax import lax
import numpy as np

D_MODEL = 1024
BATCH = 8
SEQ = 2048
DEPTH = 1
DEC_BATCH = 32
DEC_SEQ = 8
PAST_LEN = 8192
PAGE_SIZE = 128

HEAD_DIM = 64
N_HEADS_MOBA = 8
N_HEADS_SB = 8
W_MOBA = N_HEADS_MOBA * HEAD_DIM
W_SB = N_HEADS_SB * HEAD_DIM
MOBA_BLOCK = 256
MOBA_TOPK = 3
Q_BLOCK = 128
ROPE_THETA = 10000.0
D_FF = ((8 * D_MODEL // 3 + 255) // 256) * 256
RMS_EPS = 1e-6
D_IN = 3 * W_MOBA + 3 * W_SB + 2 * D_MODEL

kernel_name = 'hybrid_moba_stickbreaking_decode_step'


def rms_norm(x, g):
    xf = x.astype(jnp.float32)
    y = xf * lax.rsqrt(jnp.mean(xf * xf, axis=-1, keepdims=True) + RMS_EPS)
    return (y * g.astype(jnp.float32)).astype(x.dtype)


def rope(x, pos):
    half = HEAD_DIM // 2
    inv = ROPE_THETA ** (-jnp.arange(half, dtype=jnp.float32) / half)
    ang = pos.astype(jnp.float32)[:, None] * inv[None, :]
    cos = jnp.cos(ang)[None, :, None, :]
    sin = jnp.sin(ang)[None, :, None, :]
    xf = x.astype(jnp.float32)
    x1, x2 = xf[..., :half], xf[..., half:]
    out = jnp.concatenate([x1 * cos - x2 * sin, x2 * cos + x1 * sin], axis=-1)
    return out.astype(x.dtype)


def moba_block(q, k, v, q_pos):
    H, L, d = k.shape
    qc = q.shape[1]
    nb = L // MOBA_BLOCK
    kb = k.reshape(H, nb, MOBA_BLOCK, d)
    vb = v.reshape(H, nb, MOBA_BLOCK, d)
    k_mean = jnp.mean(kb.astype(jnp.float32), axis=2)
    cur = q_pos // MOBA_BLOCK
    gate = jnp.einsum('hqd,hnd->hqn', q.astype(jnp.float32), k_mean)
    full_past = jnp.arange(nb, dtype=jnp.int32)[None, :] < cur[:, None]
    gate = jnp.where(full_past[None], gate, -jnp.inf)
    n_sel = min(MOBA_TOPK, nb)
    _, top = lax.top_k(gate, n_sel)
    top_ok = top < cur[None, :, None]
    own = jnp.broadcast_to(cur[None, :, None], (H, qc, 1))
    sel = jnp.concatenate([top, own], axis=-1)
    sel_ok = jnp.concatenate([top_ok, jnp.ones((H, qc, 1), dtype=bool)], axis=-1)
    h_idx = jnp.arange(H)[:, None, None]
    k_sel = kb[h_idx, sel]
    v_sel = vb[h_idx, sel]
    s = jnp.einsum('hqd,hqnkd->hqnk', q, k_sel).astype(jnp.float32) / math.sqrt(d)
    k_pos = sel[..., None] * MOBA_BLOCK + jnp.arange(MOBA_BLOCK, dtype=jnp.int32)
    mask = sel_ok[..., None] & (k_pos <= q_pos[None, :, None, None])
    s = jnp.where(mask, s, -jnp.inf)
    p = jax.nn.softmax(s.reshape(H, qc, -1), axis=-1).reshape(s.shape)
    return jnp.einsum('hqnk,hqnkd->hqd', p.astype(v.dtype), v_sel)


def stick_breaking_block(q, k, v, q_pos):
    L = k.shape[1]
    z = jnp.einsum('hqd,hkd->hqk', q, k).astype(jnp.float32) / math.sqrt(HEAD_DIM)
    past = jnp.arange(L, dtype=jnp.int32)[None, :] < q_pos[:, None]
    log_keep = jnp.where(past[None], jax.nn.log_sigmoid(-z), 0.0)
    tail = lax.cumsum(log_keep, axis=2, reverse=True) - log_keep
    a = jnp.where(past[None], jnp.exp(jax.nn.log_sigmoid(z) + tail), 0.0)
    return jnp.einsum('hqk,hkd->hqd', a.astype(v.dtype), v)


def sweep_query_blocks(fn, q, k, v, q_start):
    B, H, T, d = q.shape
    qc = math.gcd(T, Q_BLOCK)
    nc = T // qc
    qb = q.reshape(B, H, nc, qc, d).transpose(0, 2, 1, 3, 4).reshape(B * nc, H, qc, d)
    idx = jnp.arange(B * nc, dtype=jnp.int32)

    def step(args):
        q_blk, i = args
        b, c = i // nc, i % nc
        q_pos = q_start + c * qc + jnp.arange(qc, dtype=jnp.int32)
        return fn(q_blk, k[b], v[b], q_pos)

    out = lax.map(step, (qb, idx))
    return out.reshape(B, nc, H, qc, d).transpose(0, 2, 1, 3, 4).reshape(B, H, T, d)


def gather_pages(cache, page_table):
    g = cache[page_table]
    nb, npg, ps, h, d = g.shape
    return g.reshape(nb, npg * ps, h, d)


def decoder_layer(x, pos_start, past, w):
    (g_mix, w_in, w_branch_moba, w_branch_sb, w_out,
     g_ffn, w_ffn_gate, w_ffn_up, w_ffn_down) = w
    B, T, _ = x.shape
    h = rms_norm(x, g_mix)
    proj = jnp.einsum('btd,de->bte', h, w_in)
    cuts = [W_MOBA, 2 * W_MOBA, 3 * W_MOBA, 3 * W_MOBA + W_SB,
            3 * W_MOBA + 2 * W_SB, 3 * W_MOBA + 3 * W_SB, 3 * W_MOBA + 3 * W_SB + D_MODEL]
    qa, ka, va, qs, ks, vs, ga, gs = jnp.split(proj, cuts, axis=-1)
    pos = pos_start + jnp.arange(T, dtype=jnp.int32)
    qa = rope(qa.reshape(B, T, N_HEADS_MOBA, HEAD_DIM), pos)
    ka = rope(ka.reshape(B, T, N_HEADS_MOBA, HEAD_DIM), pos)
    va = va.reshape(B, T, N_HEADS_MOBA, HEAD_DIM)
    qs = qs.reshape(B, T, N_HEADS_SB, HEAD_DIM)
    ks = ks.reshape(B, T, N_HEADS_SB, HEAD_DIM)
    vs = vs.reshape(B, T, N_HEADS_SB, HEAD_DIM)
    new_rows = (ka, va, ks, vs)
    if past is None:
        full = new_rows
    else:
        full = tuple(jnp.concatenate([p.astype(n.dtype), n], axis=1) for p, n in zip(past, new_rows))
    bhtd = lambda t: t.transpose(0, 2, 1, 3)
    ka_f, va_f, ks_f, vs_f = (bhtd(t) for t in full)
    L = ka_f.shape[2]
    pad = (-L) % MOBA_BLOCK
    ka_f = jnp.pad(ka_f, ((0, 0), (0, 0), (0, pad), (0, 0)))
    va_f = jnp.pad(va_f, ((0, 0), (0, 0), (0, pad), (0, 0)))
    o_a = sweep_query_blocks(moba_block, bhtd(qa), ka_f, va_f, pos_start)
    o_s = sweep_query_blocks(stick_breaking_block, bhtd(qs), ks_f, vs_f, pos_start)
    o_a = o_a.transpose(0, 2, 1, 3).reshape(B, T, W_MOBA)
    o_s = o_s.transpose(0, 2, 1, 3).reshape(B, T, W_SB)
    merged = (jax.nn.sigmoid(ga) * jnp.einsum('btw,wd->btd', o_a, w_branch_moba)
              + jax.nn.sigmoid(gs) * jnp.einsum('btw,wd->btd', o_s, w_branch_sb))
    x = x + jnp.einsum('btd,de->bte', merged, w_out)
    h2 = rms_norm(x, g_ffn)
    ff = jax.nn.silu(jnp.einsum('btd,df->btf', h2, w_ffn_gate)) * jnp.einsum('btd,df->btf', h2, w_ffn_up)
    x = x + jnp.einsum('btf,fd->btd', ff, w_ffn_down)
    return x, new_rows


def setup_inputs(seed: int = 0) -> dict:
    key = jax.random.key(seed)
    ks = jax.random.split(key, 20)
    n_pages = PAST_LEN // PAGE_SIZE
    n_used = DEC_BATCH * n_pages
    n_pool = n_used + n_used // 4
    f32 = jnp.float32

    def nrm(k, shape, scale):
        return jax.random.normal(k, shape, f32) * scale

    cshape_a = (DEPTH, n_pool, PAGE_SIZE, N_HEADS_MOBA, HEAD_DIM)
    cshape_s = (DEPTH, n_pool, PAGE_SIZE, N_HEADS_SB, HEAD_DIM)
    page_table = jax.random.permutation(ks[6], n_pool)[:n_used].reshape(DEC_BATCH, n_pages).astype(jnp.int32)
    return {
        'x_prompt': nrm(ks[0], (BATCH, SEQ, D_MODEL), 1.0),
        'x_sample': nrm(ks[1], (DEC_BATCH, DEC_SEQ, D_MODEL), 1.0),
        'cache_moba_k': nrm(ks[2], cshape_a, 1.0),
        'cache_moba_v': nrm(ks[3], cshape_a, 1.0),
        'cache_sb_k': nrm(ks[4], cshape_s, 1.0),
        'cache_sb_v': nrm(ks[5], cshape_s, 1.0),
        'page_table': page_table,
        'g_mix': 1.0 + nrm(ks[7], (DEPTH, D_MODEL), 0.02),
        'w_in': nrm(ks[8], (DEPTH, D_MODEL, D_IN), D_MODEL ** -0.5),
        'w_branch_moba': nrm(ks[9], (DEPTH, W_MOBA, D_MODEL), W_MOBA ** -0.5),
        'w_branch_sb': nrm(ks[10], (DEPTH, W_SB, D_MODEL), W_SB ** -0.5),
        'w_out': nrm(ks[11], (DEPTH, D_MODEL, D_MODEL), D_MODEL ** -0.5),
        'g_ffn': 1.0 + nrm(ks[12], (DEPTH, D_MODEL), 0.02),
        'w_ffn_gate': nrm(ks[13], (DEPTH, D_MODEL, D_FF), D_MODEL ** -0.5),
        'w_ffn_up': nrm(ks[14], (DEPTH, D_MODEL, D_FF), D_MODEL ** -0.5),
        'w_ffn_down': nrm(ks[15], (DEPTH, D_FF, D_MODEL), D_FF ** -0.5),
        'g_final': 1.0 + nrm(ks[16], (D_MODEL,), 0.02),
    }


def reference(x_prompt, x_sample, cache_moba_k, cache_moba_v, cache_sb_k, cache_sb_v,
              page_table, g_mix, w_in, w_branch_moba, w_branch_sb, w_out,
              g_ffn, w_ffn_gate, w_ffn_up, w_ffn_down, g_final):
    past_len = page_table.shape[1] * PAGE_SIZE
    xp, xs = x_prompt, x_sample
    rows_p = ([], [], [], [])
    rows_s = ([], [], [], [])
    for l in range(DEPTH):
        w = (g_mix[l], w_in[l], w_branch_moba[l], w_branch_sb[l], w_out[l],
             g_ffn[l], w_ffn_gate[l], w_ffn_up[l], w_ffn_down[l])
        xp, new_p = decoder_layer(xp, 0, None, w)
        past = (gather_pages(cache_moba_k[l], page_table), gather_pages(cache_moba_v[l], page_table),
                gather_pages(cache_sb_k[l], page_table), gather_pages(cache_sb_v[l], page_table))
        xs, new_s = decoder_layer(xs, past_len, past, w)
        for acc, r in zip(rows_p, new_p):
            acc.append(r)
        for acc, r in zip(rows_s, new_s):
            acc.append(r)
    y_prompt = rms_norm(xp, g_final)
    y_sample = rms_norm(xs, g_final)
    return (y_prompt, y_sample,
            jnp.stack(rows_p[0]), jnp.stack(rows_p[1]), jnp.stack(rows_p[2]), jnp.stack(rows_p[3]),
            jnp.stack(rows_s[0]), jnp.stack(rows_s[1]), jnp.stack(rows_s[2]), jnp.stack(rows_s[3]))
```

```python
import functools
import math

import jax
import jax.numpy as jnp
from jax import lax
from jax.experimental import pallas as pl
from jax.experimental.pallas import tpu as pltpu

F32 = jnp.float32
BF16 = jnp.bfloat16

HEAD_DIM = 64
HALF_DIM = HEAD_DIM // 2
N_HEADS = 8
W_MIX = N_HEADS * HEAD_DIM
LANES = 128
N_PAIRS = W_MIX // LANES
MOBA_BLOCK = 256
MOBA_TOPK = 3
Q_BLOCK = 128
PAGE_SIZE = 128
ROPE_THETA = 10000.0
RMS_EPS = 1e-6
NEG_BIG = -1e30
SCORE_SCALE = 1.0 / math.sqrt(HEAD_DIM)
NT_DIMS = (((1,), (1,)), ((), ()))
VMEM_LIMIT = 56 * 1024 * 1024


def _resident(shape):
    return pl.BlockSpec(shape, lambda *_: (0,) * len(shape), pipeline_mode=pl.Buffered(1))


def _log_sigmoid(z):
    return jnp.minimum(z, 0.0) - jnp.log(1.0 + jnp.exp(-jnp.abs(z)))


def _split_bf16(x):
    hi = x.astype(BF16)
    lo = (x - hi.astype(F32)).astype(BF16)
    return hi, lo


def _rms_norm_bf16(x, g):
    ms = jnp.mean(x * x, axis=-1, keepdims=True)
    return (x * lax.rsqrt(ms + RMS_EPS) * g).astype(BF16)


def _proj_body(x_ref, g_ref, wq_ref, wkv_ref, wg_ref, cos_ref, sin_ref, *refs, kv_transposed):
    if kv_transposed:
        cos_t_ref, sin_t_ref = refs[:2]
        refs = refs[2:]
    ka_ref, va_ref, ks_ref, vs_ref, qa_ref, qs_ref, ga_ref, gs_ref = refs
    h = _rms_norm_bf16(x_ref[...], g_ref[...])
    d_model = ga_ref.shape[1]

    cos = cos_ref[...]
    sin = sin_ref[...]
    lane = lax.broadcasted_iota(jnp.int32, cos.shape, 1)
    first_half = (lane % HEAD_DIM) < HALF_DIM

    def rope_slab(xs):
        rot = jnp.where(first_half, pltpu.roll(xs, LANES - HALF_DIM, 1), pltpu.roll(xs, HALF_DIM, 1))
        return xs * cos + rot * sin

    q = jnp.dot(h, wq_ref[...], preferred_element_type=F32)
    for p in range(N_PAIRS):
        sl = slice(p * LANES, (p + 1) * LANES)
        qa_ref[:, sl] = (rope_slab(q[:, sl]) * SCORE_SCALE).astype(qa_ref.dtype)
    qs_ref[...] = (q[:, W_MIX:] * SCORE_SCALE).astype(qs_ref.dtype)

    if kv_transposed:
        def seg_t(i):
            return lax.dot_general(wkv_ref[i * W_MIX:(i + 1) * W_MIX, :], h, NT_DIMS,
                                   preferred_element_type=F32)
        ka = seg_t(0)
        cos_t = cos_t_ref[...]
        sin_t = sin_t_ref[...]
        for hd in range(N_HEADS):
            x1 = ka[hd * HEAD_DIM:hd * HEAD_DIM + HALF_DIM, :]
            x2 = ka[hd * HEAD_DIM + HALF_DIM:(hd + 1) * HEAD_DIM, :]
            ka_ref[hd * HEAD_DIM:hd * HEAD_DIM + HALF_DIM, :] = x1 * cos_t - x2 * sin_t
            ka_ref[hd * HEAD_DIM + HALF_DIM:(hd + 1) * HEAD_DIM, :] = x2 * cos_t + x1 * sin_t
        va_ref[...] = seg_t(1)
        ks_ref[...] = seg_t(2)
        vs_ref[...] = seg_t(3)
    else:
        def seg(i):
            return jnp.dot(h, wkv_ref[:, i * W_MIX:(i + 1) * W_MIX], preferred_element_type=F32)
        ka = seg(0)
        for p in range(N_PAIRS):
            sl = slice(p * LANES, (p + 1) * LANES)
            ka_ref[:, sl] = rope_slab(ka[:, sl])
        va_ref[...] = seg(1)
        ks_ref[...] = seg(2)
        vs_ref[...] = seg(3)

    gates = jnp.dot(h, wg_ref[...], preferred_element_type=F32)
    ga_ref[...] = jax.nn.sigmoid(gates[:, :d_model]).astype(ga_ref.dtype)
    gs_ref[...] = jax.nn.sigmoid(gates[:, d_model:]).astype(gs_ref.dtype)


def _proj(x, g_mix, wq, wkv, wg, cos, sin, cos_t=None, sin_t=None, *, bm, seq=None):
    m, d = x.shape
    kv_transposed = seq is not None
    row = lambda width: pl.BlockSpec((bm, width), lambda i: (i, 0))
    in_specs = [row(d), _resident((1, d)), _resident(wq.shape), _resident(wkv.shape),
                _resident(wg.shape), row(LANES), row(LANES)]
    args = [x, g_mix, wq, wkv, wg, cos, sin]
    if kv_transposed:
        per_seq = seq // bm
        tab = pl.BlockSpec((HALF_DIM, bm), lambda i: (0, i % per_seq))
        in_specs += [tab, tab]
        args += [cos_t, sin_t]
        kv_shape = jax.ShapeDtypeStruct((m // seq, W_MIX, seq), F32)
        kv_spec = pl.BlockSpec((None, W_MIX, bm), lambda i: (i // per_seq, 0, i % per_seq))
    else:
        kv_shape = jax.ShapeDtypeStruct((m, W_MIX), F32)
        kv_spec = row(W_MIX)
    return pl.pallas_call(
        functools.partial(_proj_body, kv_transposed=kv_transposed),
        grid=(m // bm,),
        in_specs=in_specs,
        out_specs=[kv_spec] * 4 + [row(W_MIX)] * 2 + [row(d)] * 2,
        out_shape=[kv_shape] * 4 + [jax.ShapeDtypeStruct((m, W_MIX), BF16)] * 2
                  + [jax.ShapeDtypeStruct((m, d), BF16)] * 2,
        compiler_params=pltpu.CompilerParams(
            dimension_semantics=("arbitrary",), vmem_limit_bytes=VMEM_LIMIT),
        name="proj_prompt" if kv_transposed else "proj_decode",
    )(*args)


def _post_body(x_ref, oa_ref, os_ref, ga_ref, gs_ref, wa_ref, ws_ref, wo_ref,
               gf_ref, wg_ref, wu_ref, wd_ref, gfin_ref, y_ref, *, ff_chunk):
    merged = (ga_ref[...].astype(F32) * jnp.dot(oa_ref[...], wa_ref[...], preferred_element_type=F32)
              + gs_ref[...].astype(F32) * jnp.dot(os_ref[...], ws_ref[...], preferred_element_type=F32))
    x1 = x_ref[...] + jnp.dot(merged.astype(BF16), wo_ref[...], preferred_element_type=F32)
    h2 = _rms_norm_bf16(x1, gf_ref[...])
    d_ff = wg_ref.shape[1]
    x2 = x1
    for lo in range(0, d_ff, ff_chunk):
        gate = jnp.dot(h2, wg_ref[:, lo:lo + ff_chunk], preferred_element_type=F32)
        up = jnp.dot(h2, wu_ref[:, lo:lo + ff_chunk], preferred_element_type=F32)
        ff = (gate * jax.nn.sigmoid(gate) * up).astype(BF16)
        x2 = x2 + jnp.dot(ff, wd_ref[lo:lo + ff_chunk, :], preferred_element_type=F32)
    ms2 = jnp.mean(x2 * x2, axis=-1, keepdims=True)
    y_ref[...] = x2 * lax.rsqrt(ms2 + RMS_EPS) * gfin_ref[...]


def _post(x, oa, os_, ga, gs, wa, ws, wo, g_ffn, wg, wu, wd, g_final, *, bm):
    m, d = x.shape
    d_ff = wg.shape[1]
    row = lambda width: pl.BlockSpec((bm, width), lambda i: (i, 0))
    return pl.pallas_call(
        functools.partial(_post_body, ff_chunk=d_ff // 2),
        grid=(m // bm,),
        in_specs=[row(d), row(W_MIX), row(W_MIX), row(d), row(d),
                  _resident((W_MIX, d)), _resident((W_MIX, d)), _resident((d, d)),
                  _resident((1, d)), _resident((d, d_ff)), _resident((d, d_ff)),
                  _resident((d_ff, d)), _resident((1, d))],
        out_specs=row(d),
        out_shape=jax.ShapeDtypeStruct((m, d), F32),
        compiler_params=pltpu.CompilerParams(
            dimension_semantics=("arbitrary",), vmem_limit_bytes=VMEM_LIMIT),
        name="post_ffn",
    )(x, oa, os_, ga, gs, wa, ws, wo, g_ffn, wg, wu, wd, g_final)


GATE_GROUP = 8


def _moba_prompt_body(q_ref, k_ref, v_ref, o_ref,
                      kaug, vb, kmhi, kmlo, lhs, m_s, l_s, acc_s, *, seq):
    c = pl.program_id(1)
    nb = seq // MOBA_BLOCK
    assert nb <= GATE_GROUP and N_HEADS * GATE_GROUP <= LANES

    @pl.when(c == 0)
    def _per_sequence_setup():
        r = lax.broadcasted_iota(jnp.int32, (LANES, MOBA_BLOCK), 0)
        lane_w = lax.broadcasted_iota(jnp.int32, (W_MIX, LANES), 1)
        row_w = lax.broadcasted_iota(jnp.int32, (W_MIX, LANES), 0)
        km = jnp.zeros((W_MIX, LANES), F32)
        for t in range(nb):
            sl = slice(t * MOBA_BLOCK, (t + 1) * MOBA_BLOCK)
            ind = jnp.where((r < N_HEADS * GATE_GROUP) & ((r % GATE_GROUP) == t), 1.0, 0.0).astype(BF16)
            for p in range(N_PAIRS):
                kaug[p, t, 0:LANES, :] = k_ref[p * LANES:(p + 1) * LANES, sl].astype(BF16)
                kaug[p, t, LANES:2 * LANES, :] = ind
                vb[p, t] = v_ref[p * LANES:(p + 1) * LANES, sl].astype(BF16)
            block_sum = jnp.sum(k_ref[:, sl], axis=1, keepdims=True)
            km = jnp.where((lane_w % GATE_GROUP) == t, block_sum, km)
        km = jnp.where((lane_w < N_HEADS * GATE_GROUP) & ((row_w // HEAD_DIM) == (lane_w // GATE_GROUP)),
                       km * (1.0 / MOBA_BLOCK), 0.0)
        hi, lo = _split_bf16(km)
        kmhi[...] = hi
        kmlo[...] = lo

    q = q_ref[...]
    gate = (jnp.dot(q, kmhi[...], preferred_element_type=F32)
            + jnp.dot(q, kmlo[...], preferred_element_type=F32))
    lane = lax.broadcasted_iota(jnp.int32, (Q_BLOCK, LANES), 1)
    n = lane % GATE_GROUP
    cur = (c * Q_BLOCK) // MOBA_BLOCK
    g = jnp.where(n < cur, gate, -jnp.inf)
    rank = jnp.zeros((Q_BLOCK, LANES), F32)
    for r in range(1, GATE_GROUP):
        wraps = (n + r) >= GATE_GROUP
        other = jnp.where(wraps, pltpu.roll(g, GATE_GROUP - r, 1), pltpu.roll(g, LANES - r, 1))
        beats = (other > g) | (wraps & (other == g))
        rank = rank + beats.astype(F32)
    keep = ((n < cur) & (rank < MOBA_TOPK)) | (n >= cur)
    bias = jnp.where(keep | (lane >= N_HEADS * GATE_GROUP), 0.0, NEG_BIG)

    qf = q.astype(F32)
    for h in range(N_HEADS):
        p, j = divmod(h, 2)
        in_head = (lane >= j * HEAD_DIM) & (lane < (j + 1) * HEAD_DIM)
        lhs[h, :, 0:LANES] = jnp.where(in_head, qf[:, p * LANES:(p + 1) * LANES], 0.0).astype(BF16)
        in_group = (lane >= h * GATE_GROUP) & (lane < (h + 1) * GATE_GROUP)
        lhs[h, :, LANES:2 * LANES] = jnp.where(in_group, bias, 0.0).astype(BF16)
    m_s[...] = jnp.full(m_s.shape, NEG_BIG, F32)
    l_s[...] = jnp.zeros(l_s.shape, F32)
    acc_s[...] = jnp.zeros(acc_s.shape, F32)

    qq = lax.broadcasted_iota(jnp.int32, (Q_BLOCK, MOBA_BLOCK), 0)
    kk = lax.broadcasted_iota(jnp.int32, (Q_BLOCK, MOBA_BLOCK), 1)
    causal = kk <= qq + (c * Q_BLOCK - cur * MOBA_BLOCK)

    def tile_step(t, own):
        for p in range(N_PAIRS):
            kt = kaug[p, t]
            vt = vb[p, t]
            for j in range(2):
                h = 2 * p + j
                s = jnp.dot(lhs[h], kt, preferred_element_type=F32)
                if own:
                    s = jnp.where(causal, s, NEG_BIG)
                m_old = m_s[h]
                m_new = jnp.maximum(m_old, jnp.max(s, axis=1, keepdims=True))
                alpha = jnp.exp(m_old - m_new)
                pr = jnp.exp(s - m_new)
                l_s[h] = alpha * l_s[h] + jnp.sum(pr, axis=1, keepdims=True)
                m_s[h] = m_new
                acc_s[h] = alpha * acc_s[h] + lax.dot_general(
                    pr.astype(BF16), vt, NT_DIMS, preferred_element_type=F32)

    def past_tile(t, carry):
        tile_step(t, own=False)
        return carry

    lax.fori_loop(0, cur, past_tile, 0)
    tile_step(cur, own=True)

    for p in range(N_PAIRS):
        oa = acc_s[2 * p] / l_s[2 * p]
        ob = acc_s[2 * p + 1] / l_s[2 * p + 1]
        o_ref[:, p * LANES:(p + 1) * LANES] = jnp.where(lane < HEAD_DIM, oa, ob).astype(o_ref.dtype)


def _moba_prompt(qa, ka_t, va_t, *, batch, seq):
    nq = seq // Q_BLOCK
    nb = seq // MOBA_BLOCK
    kv_spec = pl.BlockSpec((None, W_MIX, seq), lambda b, c: (b, 0, 0))
    return pl.pallas_call(
        functools.partial(_moba_prompt_body, seq=seq),
        grid=(batch, nq),
        in_specs=[pl.BlockSpec((Q_BLOCK, W_MIX), lambda b, c: (b * nq + c, 0)), kv_spec, kv_spec],
        out_specs=pl.BlockSpec((Q_BLOCK, W_MIX), lambda b, c: (b * nq + c, 0)),
        out_shape=jax.ShapeDtypeStruct((batch * seq, W_MIX), BF16),
        scratch_shapes=[pltpu.VMEM((N_PAIRS, nb, 2 * LANES, MOBA_BLOCK), BF16),
                        pltpu.VMEM((N_PAIRS, nb, LANES, MOBA_BLOCK), BF16),
                        pltpu.VMEM((W_MIX, LANES), BF16),
                        pltpu.VMEM((W_MIX, LANES), BF16),
                        pltpu.VMEM((N_HEADS, Q_BLOCK, 2 * LANES), BF16),
                        pltpu.VMEM((N_HEADS, Q_BLOCK, 1), F32),
                        pltpu.VMEM((N_HEADS, Q_BLOCK, 1), F32),
                        pltpu.VMEM((N_HEADS, Q_BLOCK, LANES), F32)],
        compiler_params=pltpu.CompilerParams(
            dimension_semantics=("arbitrary", "arbitrary"), vmem_limit_bytes=VMEM_LIMIT),
        name="moba_prompt",
    )(qa, ka_t, va_t)


SB_TILE = 256


def _strict_upper_ones(nrows):
    r = lax.broadcasted_iota(jnp.int32, (nrows, nrows), 0)
    c = lax.broadcasted_iota(jnp.int32, (nrows, nrows), 1)
    return jnp.where(r > c, 1.0, 0.0).astype(BF16)


def _sb_weights(z, past, later, tri):
    ls = _log_sigmoid(z)
    lk = ls - z
    if past is not None:
        lk = jnp.where(past, lk, 0.0)
    hi, lo = _split_bf16(lk)
    tail = (jnp.dot(hi, tri, preferred_element_type=F32)
            + jnp.dot(lo, tri, preferred_element_type=F32))
    a = jnp.exp(ls + tail + later)
    if past is not None:
        a = jnp.where(past, a, 0.0)
    return a, jnp.sum(lk, axis=1, keepdims=True)


def _sb_prompt_body(q_ref, k_ref, v_ref, o_ref, kb, vb, tri, later_s, acc_s, *, seq):
    c = pl.program_id(2)

    @pl.when(c == 0)
    def _per_sequence_setup():
        for t in range(seq // SB_TILE):
            sl = slice(t * SB_TILE, (t + 1) * SB_TILE)
            kb[t] = k_ref[:, sl].astype(BF16)
            vb[t] = v_ref[:, sl].astype(BF16)
        tri[...] = _strict_upper_ones(SB_TILE)

    qf = q_ref[...].astype(F32)
    lane = lax.broadcasted_iota(jnp.int32, (Q_BLOCK, LANES), 1)
    qm = [jnp.where((lane >= j * HEAD_DIM) & (lane < (j + 1) * HEAD_DIM), qf, 0.0).astype(BF16)
          for j in range(2)]
    later_s[...] = jnp.zeros(later_s.shape, F32)
    acc_s[...] = jnp.zeros(acc_s.shape, F32)

    diag = (c * Q_BLOCK) // SB_TILE
    qq = lax.broadcasted_iota(jnp.int32, (Q_BLOCK, SB_TILE), 0)
    kk = lax.broadcasted_iota(jnp.int32, (Q_BLOCK, SB_TILE), 1)
    past_diag = kk < qq + (c * Q_BLOCK - diag * SB_TILE)

    def tile_step(t, past):
        kt = kb[t]
        vt = vb[t]
        for j in range(2):
            z = jnp.dot(qm[j], kt, preferred_element_type=F32)
            a, tot = _sb_weights(z, past, later_s[j], tri[...])
            acc_s[j] = acc_s[j] + lax.dot_general(a.astype(BF16), vt, NT_DIMS,
                                                  preferred_element_type=F32)
            later_s[j] = later_s[j] + tot

    tile_step(diag, past_diag)

    def earlier_tile(i, carry):
        tile_step(diag - 1 - i, None)
        return carry

    lax.fori_loop(0, diag, earlier_tile, 0)
    o_ref[...] = jnp.where(lane < HEAD_DIM, acc_s[0], acc_s[1]).astype(o_ref.dtype)


def _sb_prompt(qs, ks_t, vs_t, *, batch, seq):
    nq = seq // Q_BLOCK
    nt = seq // SB_TILE
    kv_spec = pl.BlockSpec((None, LANES, seq), lambda b, p, c: (b, p, 0))
    return pl.pallas_call(
        functools.partial(_sb_prompt_body, seq=seq),
        grid=(batch, N_PAIRS, nq),
        in_specs=[pl.BlockSpec((Q_BLOCK, LANES), lambda b, p, c: (b * nq + c, p)),
                  kv_spec, kv_spec],
        out_specs=pl.BlockSpec((Q_BLOCK, LANES), lambda b, p, c: (b * nq + c, p)),
        out_shape=jax.ShapeDtypeStruct((batch * seq, W_MIX), BF16),
        scratch_shapes=[pltpu.VMEM((nt, LANES, SB_TILE), BF16),
                        pltpu.VMEM((nt, LANES, SB_TILE), BF16),
                        pltpu.VMEM((SB_TILE, SB_TILE), BF16),
                        pltpu.VMEM((2, Q_BLOCK, 1), F32),
                        pltpu.VMEM((2, Q_BLOCK, LANES), F32)],
        compiler_params=pltpu.CompilerParams(
            dimension_semantics=("arbitrary", "arbitrary", "arbitrary"),
            vmem_limit_bytes=VMEM_LIMIT),
        name="sb_prompt",
    )(qs, ks_t, vs_t)


PAGES_PER_STEP = 8
STEP_KEYS = PAGES_PER_STEP * PAGE_SIZE


def _decode_attn_body(pt_ref, q_ref, kn_ref, vn_ref, *refs, mode, n_groups, dec_seq):
    del pt_ref
    npg = PAGES_PER_STEP
    k_pages = refs[:npg]
    v_pages = refs[npg:2 * npg]
    o_ref = refs[2 * npg]
    qbd, z_s, a_s, anew_s, norm_s, o_acc = refs[2 * npg + 1:]
    ph = pl.program_id(1)
    g = pl.program_id(2)
    rows = N_HEADS * dec_seq
    assert rows <= LANES and dec_seq <= 8

    row_i = lax.broadcasted_iota(jnp.int32, (rows, LANES), 0) % dec_seq
    lane = lax.broadcasted_iota(jnp.int32, (rows, LANES), 1)

    @pl.when((ph == 0) & (g == 0))
    def _build_block_diagonal_queries():
        qf = q_ref[...].astype(F32)
        qt = jnp.concatenate([qf] * N_HEADS, axis=0)
        r = lax.broadcasted_iota(jnp.int32, qt.shape, 0)
        l = lax.broadcasted_iota(jnp.int32, qt.shape, 1)
        qbd[...] = jnp.where((l // HEAD_DIM) == (r // dec_seq), qt, 0.0).astype(BF16)

    @pl.when(ph == 0)
    def _score_pages():
        keys = jnp.concatenate([k_pages[i][...].astype(BF16) for i in range(npg)], axis=1)
        z_s[g] = jnp.dot(qbd[...], keys, preferred_element_type=F32)

    def new_token_scores():
        kn = jnp.concatenate([kn_ref[...], jnp.zeros((LANES - dec_seq, W_MIX), F32)], axis=0)
        return lax.dot_general(qbd[...], kn.astype(BF16), NT_DIMS, preferred_element_type=F32)

    @pl.when((ph == 0) & (g == n_groups - 1))
    def _weights():
        zn = new_token_scores()
        if mode == "sb":
            tri = _strict_upper_ones(SB_TILE)
            past_new = lane < row_i
            a_new, later = _sb_weights(zn, past_new, 0.0, tri[:LANES, :LANES])
            anew_s[...] = a_new.astype(BF16)
            for gg in range(n_groups - 1, -1, -1):
                for t in range(STEP_KEYS // SB_TILE - 1, -1, -1):
                    sl = slice(t * SB_TILE, (t + 1) * SB_TILE)
                    a, tot = _sb_weights(z_s[gg, :, sl], None, later, tri)
                    a_s[gg, :, sl] = a.astype(BF16)
                    later = later + tot
            norm_s[...] = jnp.ones(norm_s.shape, F32)
        else:
            blocks_per_step = STEP_KEYS // MOBA_BLOCK
            n_blocks = n_groups * blocks_per_step
            assert n_blocks <= LANES
            gate = jnp.zeros((rows, LANES), F32)
            for gg in range(n_groups):
                for t in range(blocks_per_step):
                    blk = z_s[gg, :, t * MOBA_BLOCK:(t + 1) * MOBA_BLOCK]
                    gate = jnp.where(lane == gg * blocks_per_step + t,
                                     jnp.sum(blk, axis=1, keepdims=True), gate)
            gm = jnp.where(lane < n_blocks, gate, -jnp.inf)
            sel = jnp.zeros((rows, LANES), F32)
            for _ in range(min(MOBA_TOPK, n_blocks)):
                mx = jnp.max(gm, axis=1, keepdims=True)
                idx = jnp.min(jnp.where(gm == mx, lane, LANES), axis=1, keepdims=True)
                pick = lane == idx
                sel = jnp.where(pick, 1.0, sel)
                gm = jnp.where(pick, -jnp.inf, gm)
            sel_b = sel.astype(BF16)
            own = (lane <= row_i) & (lane < dec_seq)
            zn = jnp.where(own, zn, NEG_BIG)
            mx = jnp.max(zn, axis=1, keepdims=True)
            masked = []
            for gg in range(n_groups):
                bl = lax.broadcasted_iota(jnp.int32, (LANES, STEP_KEYS), 0)
                bk = lax.broadcasted_iota(jnp.int32, (LANES, STEP_KEYS), 1)
                expand = jnp.where(bl == gg * blocks_per_step + bk // MOBA_BLOCK, 1.0, 0.0).astype(BF16)
                chosen = jnp.dot(sel_b, expand, preferred_element_type=F32) > 0.5
                s = jnp.where(chosen, z_s[gg], NEG_BIG)
                masked.append(s)
                mx = jnp.maximum(mx, jnp.max(s, axis=1, keepdims=True))
            p_new = jnp.exp(zn - mx)
            denom = jnp.sum(p_new, axis=1, keepdims=True)
            anew_s[...] = p_new.astype(BF16)
            for gg in range(n_groups):
                pr = jnp.exp(masked[gg] - mx)
                denom = denom + jnp.sum(pr, axis=1, keepdims=True)
                a_s[gg] = pr.astype(BF16)
            norm_s[...] = denom

    @pl.when((ph == 1) & (g == 0))
    def _new_token_values():
        vn = jnp.concatenate([vn_ref[...], jnp.zeros((LANES - dec_seq, W_MIX), F32)], axis=0)
        o_acc[...] = jnp.dot(anew_s[...], vn.astype(BF16), preferred_element_type=F32)

    @pl.when(ph == 1)
    def _weighted_values():
        vals = jnp.concatenate([v_pages[i][...].astype(BF16) for i in range(npg)], axis=1)
        o_acc[...] = o_acc[...] + lax.dot_general(a_s[g], vals, NT_DIMS, preferred_element_type=F32)

    @pl.when((ph == 1) & (g == n_groups - 1))
    def _write_out():
        o = o_acc[...] / norm_s[...]
        r = lax.broadcasted_iota(jnp.int32, (dec_seq, W_MIX), 1) // HEAD_DIM
        out = jnp.zeros((dec_seq, W_MIX), F32)
        for h in range(N_HEADS):
            out = jnp.where(r == h, o[h * dec_seq:(h + 1) * dec_seq, :], out)
        o_ref[...] = out.astype(o_ref.dtype)


def _decode_attn(page_table_flat, q, k_new, v_new, cache_k, cache_v, *, mode, n_pages):
    dec_batch, dec_seq, _ = q.shape
    npg = PAGES_PER_STEP
    n_groups = n_pages // npg
    rows = N_HEADS * dec_seq
    small = pl.BlockSpec((None, dec_seq, W_MIX), lambda b, ph, g, pt: (b, 0, 0))

    def k_page(i):
        def index(b, ph, g, pt):
            page = jnp.where(ph == 0, g * npg + i, n_pages - npg + i)
            return (pt[b * n_pages + page], 0, 0)
        return pl.BlockSpec((None, W_MIX, PAGE_SIZE), index)

    def v_page(i):
        def index(b, ph, g, pt):
            page = jnp.where(ph == 1, g * npg + i, i)
            return (pt[b * n_pages + page], 0, 0)
        return pl.BlockSpec((None, W_MIX, PAGE_SIZE), index)

    grid_spec = pltpu.PrefetchScalarGridSpec(
        num_scalar_prefetch=1,
        grid=(dec_batch, 2, n_groups),
        in_specs=[small, small, small] + [k_page(i) for i in range(npg)]
                 + [v_page(i) for i in range(npg)],
        out_specs=small,
        scratch_shapes=[pltpu.VMEM((rows, W_MIX), BF16),
                        pltpu.VMEM((n_groups, rows, STEP_KEYS), F32),
                        pltpu.VMEM((n_groups, rows, STEP_KEYS), BF16),
                        pltpu.VMEM((rows, LANES), BF16),
                        pltpu.VMEM((rows, 1), F32),
                        pltpu.VMEM((rows, W_MIX), F32)],
    )
    return pl.pallas_call(
        functools.partial(_decode_attn_body, mode=mode, n_groups=n_groups, dec_seq=dec_seq),
        grid_spec=grid_spec,
        out_shape=jax.ShapeDtypeStruct((dec_batch, dec_seq, W_MIX), BF16),
        compiler_params=pltpu.CompilerParams(
            dimension_semantics=("arbitrary", "arbitrary", "arbitrary"),
            vmem_limit_bytes=VMEM_LIMIT),
        name="decode_" + mode,
    )(page_table_flat, q, k_new, v_new, *([cache_k] * npg), *([cache_v] * npg))


def _rope_tables(pos):
    inv = ROPE_THETA ** (-jnp.arange(HALF_DIM, dtype=F32) / HALF_DIM)
    ang = pos.astype(F32)[:, None] * inv[None, :]
    cos, sin = jnp.cos(ang), jnp.sin(ang)
    cos_lane = jnp.tile(cos, (1, LANES // HALF_DIM))
    sin_lane = jnp.tile(jnp.concatenate([-sin, sin], axis=1), (1, LANES // HEAD_DIM))
    return cos_lane, sin_lane, cos.T, sin.T


def kernel(x_prompt, x_sample, cache_moba_k, cache_moba_v, cache_sb_k, cache_sb_v, page_table,
           g_mix, w_in, w_branch_moba, w_branch_sb, w_out, g_ffn, w_ffn_gate, w_ffn_up,
           w_ffn_down, g_final):
    batch, seq, d = x_prompt.shape
    dec_batch, dec_seq, _ = x_sample.shape
    depth = w_in.shape[0]
    n_pages = page_table.shape[1]
    past_len = n_pages * PAGE_SIZE
    n_pool = cache_moba_k.shape[1]
    assert seq % 512 == 0 and past_len % MOBA_BLOCK == 0 and n_pages % PAGES_PER_STEP == 0
    assert depth == 1

    cos_p, sin_p, cos_pt, sin_pt = _rope_tables(jnp.arange(seq, dtype=jnp.int32))
    cos_p, sin_p = jnp.tile(cos_p, (batch, 1)), jnp.tile(sin_p, (batch, 1))
    cos_s, sin_s, _, _ = _rope_tables(past_len + jnp.arange(dec_seq, dtype=jnp.int32))
    cos_s, sin_s = jnp.tile(cos_s, (dec_batch, 1)), jnp.tile(sin_s, (dec_batch, 1))
    pt_flat = page_table.reshape(-1)

    xp = x_prompt.reshape(batch * seq, d)
    xs = x_sample.reshape(dec_batch * dec_seq, d)
    m_s = dec_batch * dec_seq
    g_last = g_final.reshape(1, d)
    l = 0
    w_in_b = w_in[l].astype(BF16)
    wq = jnp.concatenate([w_in_b[:, 0:W_MIX], w_in_b[:, 3 * W_MIX:4 * W_MIX]], axis=1)
    wkv = jnp.concatenate([w_in_b[:, W_MIX:3 * W_MIX], w_in_b[:, 4 * W_MIX:6 * W_MIX]], axis=1)
    wgt = w_in_b[:, 6 * W_MIX:]
    bf = lambda w: w[l].astype(BF16)
    wa, ws, wo = bf(w_branch_moba), bf(w_branch_sb), bf(w_out)
    wg, wu, wd = bf(w_ffn_gate), bf(w_ffn_up), bf(w_ffn_down)
    gm, gf = g_mix[l].reshape(1, d), g_ffn[l].reshape(1, d)

    ka_t, va_t, ks_t, vs_t, qa, qs, ga, gs = _proj(
        xp, gm, wq, wkv.T, wgt, cos_p, sin_p, cos_pt, sin_pt, bm=512, seq=seq)
    oa = _moba_prompt(qa, ka_t, va_t, batch=batch, seq=seq)
    os_ = _sb_prompt(qs, ks_t, vs_t, batch=batch, seq=seq)
    y_prompt = _post(xp, oa, os_, ga, gs, wa, ws, wo, gf, wg, wu, wd, g_last, bm=512)

    ka2, va2, ks2, vs2, qa2, qs2, ga2, gs2 = _proj(xs, gm, wq, wkv, wgt, cos_s, sin_s, bm=m_s)
    r3 = lambda t: t.reshape(dec_batch, dec_seq, W_MIX)
    pool = lambda cch: cch[l].transpose(0, 2, 3, 1).reshape(n_pool, W_MIX, PAGE_SIZE)
    oa2 = _decode_attn(pt_flat, r3(qa2), r3(ka2), r3(va2), pool(cache_moba_k),
                       pool(cache_moba_v), mode="moba", n_pages=n_pages)
    os2 = _decode_attn(pt_flat, r3(qs2), r3(ks2), r3(vs2), pool(cache_sb_k),
                       pool(cache_sb_v), mode="sb", n_pages=n_pages)
    y_sample = _post(xs, oa2.reshape(m_s, W_MIX), os2.reshape(m_s, W_MIX), ga2, gs2,
                     wa, ws, wo, gf, wg, wu, wd, g_last, bm=m_s)

    rows_p = lambda t: t.reshape(1, batch, N_HEADS, HEAD_DIM, seq).transpose(0, 1, 4, 2, 3)
    rows_s = lambda t: t.reshape(1, dec_batch, dec_seq, N_HEADS, HEAD_DIM)
    return (y_prompt.reshape(batch, seq, d), y_sample.reshape(dec_batch, dec_seq, d),
            rows_p(ka_t), rows_p(va_t), rows_p(ks_t), rows_p(vs_t),
            rows_s(ka2), rows_s(va2), rows_s(ks2), rows_s(vs2))
```

```python
import functools
import math

import jax
import jax.numpy as jnp
from jax import lax
from jax.experimental import pallas as pl
from jax.experimental.pallas import tpu as pltpu

F32 = jnp.float32
BF16 = jnp.bfloat16

HEAD_DIM = 64
HALF_DIM = HEAD_DIM // 2
N_HEADS = 8
W_MIX = N_HEADS * HEAD_DIM
LANES = 128
N_PAIRS = W_MIX // LANES
MOBA_BLOCK = 256
MOBA_TOPK = 3
Q_BLOCK = 128
PAGE_SIZE = 128
ROPE_THETA = 10000.0
RMS_EPS = 1e-6
NEG_BIG = -1e30
SCORE_SCALE = math.log2(math.e) / math.sqrt(HEAD_DIM)
NT_DIMS = (((1,), (1,)), ((), ()))
VMEM_LIMIT = 56 * 1024 * 1024


def _resident(shape):
    return pl.BlockSpec(shape, lambda *_: (0,) * len(shape), pipeline_mode=pl.Buffered(1))


def _log2_sigmoid(z2):
    return jnp.minimum(z2, 0.0) - jnp.log2(1.0 + jnp.exp2(-jnp.abs(z2)))


def _split_bf16(x):
    hi = x.astype(BF16)
    lo = (x - hi.astype(F32)).astype(BF16)
    return hi, lo


def _rms_norm_bf16(x, g):
    ms = jnp.mean(x * x, axis=-1, keepdims=True)
    return (x * lax.rsqrt(ms + RMS_EPS) * g).astype(BF16)


def _proj_body(x_ref, g_ref, wq_ref, wkv_ref, wg_ref, cos_ref, sin_ref, *refs, kv_transposed):
    if kv_transposed:
        cos_t_ref, sin_t_ref = refs[:2]
        refs = refs[2:]
    ka_ref, va_ref, ks_ref, vs_ref, qa_ref, qs_ref, ga_ref, gs_ref = refs
    h = _rms_norm_bf16(x_ref[...], g_ref[...])
    d_model = ga_ref.shape[1]

    cos = cos_ref[...]
    sin = sin_ref[...]
    lane = lax.broadcasted_iota(jnp.int32, cos.shape, 1)
    first_half = (lane % HEAD_DIM) < HALF_DIM

    def rope_slab(xs):
        rot = jnp.where(first_half, pltpu.roll(xs, LANES - HALF_DIM, 1), pltpu.roll(xs, HALF_DIM, 1))
        return xs * cos + rot * sin

    q = jnp.dot(h, wq_ref[...], preferred_element_type=F32)
    for p in range(N_PAIRS):
        sl = slice(p * LANES, (p + 1) * LANES)
        qa_ref[:, sl] = (rope_slab(q[:, sl]) * SCORE_SCALE).astype(qa_ref.dtype)
    qs_ref[...] = (q[:, W_MIX:] * SCORE_SCALE).astype(qs_ref.dtype)

    if kv_transposed:
        def seg_t(i):
            return lax.dot_general(wkv_ref[i * W_MIX:(i + 1) * W_MIX, :], h, NT_DIMS,
                                   preferred_element_type=F32)
        ka = seg_t(0)
        cos_t = cos_t_ref[...]
        sin_t = sin_t_ref[...]
        for hd in range(N_HEADS):
            x1 = ka[hd * HEAD_DIM:hd * HEAD_DIM + HALF_DIM, :]
            x2 = ka[hd * HEAD_DIM + HALF_DIM:(hd + 1) * HEAD_DIM, :]
            ka_ref[hd * HEAD_DIM:hd * HEAD_DIM + HALF_DIM, :] = x1 * cos_t - x2 * sin_t
            ka_ref[hd * HEAD_DIM + HALF_DIM:(hd + 1) * HEAD_DIM, :] = x2 * cos_t + x1 * sin_t
        va_ref[...] = seg_t(1)
        ks_ref[...] = seg_t(2)
        vs_ref[...] = seg_t(3)
    else:
        def seg(i):
            return jnp.dot(h, wkv_ref[:, i * W_MIX:(i + 1) * W_MIX], preferred_element_type=F32)
        ka = seg(0)
        for p in range(N_PAIRS):
            sl = slice(p * LANES, (p + 1) * LANES)
            ka_ref[:, sl] = rope_slab(ka[:, sl])
        va_ref[...] = seg(1)
        ks_ref[...] = seg(2)
        vs_ref[...] = seg(3)

    gates = jnp.dot(h, wg_ref[...], preferred_element_type=F32)
    ga_ref[...] = jax.nn.sigmoid(gates[:, :d_model]).astype(ga_ref.dtype)
    gs_ref[...] = jax.nn.sigmoid(gates[:, d_model:]).astype(gs_ref.dtype)


def _proj(x, g_mix, wq, wkv, wg, cos, sin, cos_t=None, sin_t=None, *, bm, seq=None):
    m, d = x.shape
    kv_transposed = seq is not None
    row = lambda width: pl.BlockSpec((bm, width), lambda i: (i, 0))
    in_specs = [row(d), _resident((1, d)), _resident(wq.shape), _resident(wkv.shape),
                _resident(wg.shape), row(LANES), row(LANES)]
    args = [x, g_mix, wq, wkv, wg, cos, sin]
    if kv_transposed:
        per_seq = seq // bm
        tab = pl.BlockSpec((HALF_DIM, bm), lambda i: (0, i % per_seq))
        in_specs += [tab, tab]
        args += [cos_t, sin_t]
        kv_shape = jax.ShapeDtypeStruct((m // seq, W_MIX, seq), F32)
        kv_spec = pl.BlockSpec((None, W_MIX, bm), lambda i: (i // per_seq, 0, i % per_seq))
    else:
        kv_shape = jax.ShapeDtypeStruct((m, W_MIX), F32)
        kv_spec = row(W_MIX)
    return pl.pallas_call(
        functools.partial(_proj_body, kv_transposed=kv_transposed),
        grid=(m // bm,),
        in_specs=in_specs,
        out_specs=[kv_spec] * 4 + [row(W_MIX)] * 2 + [row(d)] * 2,
        out_shape=[kv_shape] * 4 + [jax.ShapeDtypeStruct((m, W_MIX), BF16)] * 2
                  + [jax.ShapeDtypeStruct((m, d), BF16)] * 2,
        compiler_params=pltpu.CompilerParams(
            dimension_semantics=("arbitrary",), vmem_limit_bytes=VMEM_LIMIT),
        name="proj_prompt" if kv_transposed else "proj_decode",
    )(*args)


def _post_body(x_ref, oa_ref, os_ref, ga_ref, gs_ref, wa_ref, ws_ref, wo_ref,
               gf_ref, wg_ref, wu_ref, wd_ref, gfin_ref, y_ref, *, ff_chunk):
    merged = (ga_ref[...].astype(F32) * jnp.dot(oa_ref[...], wa_ref[...], preferred_element_type=F32)
              + gs_ref[...].astype(F32) * jnp.dot(os_ref[...], ws_ref[...], preferred_element_type=F32))
    x1 = x_ref[...] + jnp.dot(merged.astype(BF16), wo_ref[...], preferred_element_type=F32)
    h2 = _rms_norm_bf16(x1, gf_ref[...])
    d_ff = wg_ref.shape[1]
    x2 = x1
    for lo in range(0, d_ff, ff_chunk):
        gate = jnp.dot(h2, wg_ref[:, lo:lo + ff_chunk], preferred_element_type=F32)
        up = jnp.dot(h2, wu_ref[:, lo:lo + ff_chunk], preferred_element_type=F32)
        ff = (gate * jax.nn.sigmoid(gate) * up).astype(BF16)
        x2 = x2 + jnp.dot(ff, wd_ref[lo:lo + ff_chunk, :], preferred_element_type=F32)
    ms2 = jnp.mean(x2 * x2, axis=-1, keepdims=True)
    y_ref[...] = x2 * lax.rsqrt(ms2 + RMS_EPS) * gfin_ref[...]


def _post(x, oa, os_, ga, gs, wa, ws, wo, g_ffn, wg, wu, wd, g_final, *, bm):
    m, d = x.shape
    d_ff = wg.shape[1]
    row = lambda width: pl.BlockSpec((bm, width), lambda i: (i, 0))
    return pl.pallas_call(
        functools.partial(_post_body, ff_chunk=d_ff // 2),
        grid=(m // bm,),
        in_specs=[row(d), row(W_MIX), row(W_MIX), row(d), row(d),
                  _resident((W_MIX, d)), _resident((W_MIX, d)), _resident((d, d)),
                  _resident((1, d)), _resident((d, d_ff)), _resident((d, d_ff)),
                  _resident((d_ff, d)), _resident((1, d))],
        out_specs=row(d),
        out_shape=jax.ShapeDtypeStruct((m, d), F32),
        compiler_params=pltpu.CompilerParams(
            dimension_semantics=("arbitrary",), vmem_limit_bytes=VMEM_LIMIT),
        name="post_ffn",
    )(x, oa, os_, ga, gs, wa, ws, wo, g_ffn, wg, wu, wd, g_final)


GATE_GROUP = 8


def _moba_prompt_body(q_ref, k_ref, v_ref, o_ref,
                      kaug, vb, kmhi, kmlo, lhs, m_s, acc_s, *, seq):
    c = pl.program_id(1)
    nb = seq // MOBA_BLOCK
    assert nb <= GATE_GROUP and N_HEADS * GATE_GROUP <= LANES

    @pl.when(c == 0)
    def _per_sequence_setup():
        r = lax.broadcasted_iota(jnp.int32, (LANES, MOBA_BLOCK), 0)
        lane_w = lax.broadcasted_iota(jnp.int32, (W_MIX, LANES), 1)
        row_w = lax.broadcasted_iota(jnp.int32, (W_MIX, LANES), 0)
        km = jnp.zeros((W_MIX, LANES), F32)
        for t in range(nb):
            sl = slice(t * MOBA_BLOCK, (t + 1) * MOBA_BLOCK)
            ind = jnp.where((r < N_HEADS * GATE_GROUP) & ((r % GATE_GROUP) == t), 1.0, 0.0).astype(BF16)
            for p in range(N_PAIRS):
                kaug[p, t, 0:LANES, :] = k_ref[p * LANES:(p + 1) * LANES, sl].astype(BF16)
                kaug[p, t, LANES:2 * LANES, :] = ind
                vb[p, t, 0:LANES, :] = v_ref[p * LANES:(p + 1) * LANES, sl].astype(BF16)
                vb[p, t, LANES:2 * LANES, :] = jnp.ones((LANES, MOBA_BLOCK), BF16)
            block_sum = jnp.sum(k_ref[:, sl], axis=1, keepdims=True)
            km = jnp.where((lane_w % GATE_GROUP) == t, block_sum, km)
        km = jnp.where((lane_w < N_HEADS * GATE_GROUP) & ((row_w // HEAD_DIM) == (lane_w // GATE_GROUP)),
                       km * (1.0 / MOBA_BLOCK), 0.0)
        hi, lo = _split_bf16(km)
        kmhi[...] = hi
        kmlo[...] = lo

    q = q_ref[...]
    gate = (jnp.dot(q, kmhi[...], preferred_element_type=F32)
            + jnp.dot(q, kmlo[...], preferred_element_type=F32))
    lane = lax.broadcasted_iota(jnp.int32, (Q_BLOCK, LANES), 1)
    n = lane % GATE_GROUP
    cur = (c * Q_BLOCK) // MOBA_BLOCK
    g = jnp.where(n < cur, gate, -jnp.inf)
    rank = jnp.zeros((Q_BLOCK, LANES), F32)
    for r in range(1, GATE_GROUP):
        wraps = (n + r) >= GATE_GROUP
        other = jnp.where(wraps, pltpu.roll(g, GATE_GROUP - r, 1), pltpu.roll(g, LANES - r, 1))
        beats = (other > g) | (wraps & (other == g))
        rank = rank + beats.astype(F32)
    keep = ((n < cur) & (rank < MOBA_TOPK)) | (n >= cur)
    bias = jnp.where(keep | (lane >= N_HEADS * GATE_GROUP), 0.0, NEG_BIG)

    qf = q.astype(F32)
    for h in range(N_HEADS):
        p, j = divmod(h, 2)
        in_head = (lane >= j * HEAD_DIM) & (lane < (j + 1) * HEAD_DIM)
        lhs[h, :, 0:LANES] = jnp.where(in_head, qf[:, p * LANES:(p + 1) * LANES], 0.0).astype(BF16)
        in_group = (lane >= h * GATE_GROUP) & (lane < (h + 1) * GATE_GROUP)
        lhs[h, :, LANES:2 * LANES] = jnp.where(in_group, bias, 0.0).astype(BF16)
    m_s[...] = jnp.full(m_s.shape, NEG_BIG, F32)
    acc_s[...] = jnp.zeros(acc_s.shape, F32)

    qq = lax.broadcasted_iota(jnp.int32, (Q_BLOCK, MOBA_BLOCK), 0)
    kk = lax.broadcasted_iota(jnp.int32, (Q_BLOCK, MOBA_BLOCK), 1)
    causal = kk <= qq + (c * Q_BLOCK - cur * MOBA_BLOCK)
    twice = lambda x: jnp.concatenate([x, x], axis=1)

    def tile_step(t, own):
        heads = range(N_HEADS)
        s = [jnp.dot(lhs[h], kaug[h // 2, t], preferred_element_type=F32) for h in heads]
        if own:
            s = [jnp.where(causal, x, NEG_BIG) for x in s]
        m_old = [m_s[h] for h in heads]
        m_new = [jnp.maximum(m_old[h], jnp.max(s[h], axis=1, keepdims=True)) for h in heads]
        alpha = [jnp.exp2(m_old[h] - m_new[h]) for h in heads]
        pr = [jnp.exp2(s[h] - twice(m_new[h])).astype(BF16) for h in heads]
        for h in heads:
            pv = lax.dot_general(pr[h], vb[h // 2, t], NT_DIMS, preferred_element_type=F32)
            acc_s[h] = twice(alpha[h]) * acc_s[h] + pv
            m_s[h] = m_new[h]

    def past_tile(t, carry):
        tile_step(t, own=False)
        return carry

    lax.fori_loop(0, cur, past_tile, 0)
    tile_step(cur, own=True)

    for p in range(N_PAIRS):
        acc_a, acc_b = acc_s[2 * p], acc_s[2 * p + 1]
        oa = acc_a[:, :LANES] / acc_a[:, LANES:]
        ob = acc_b[:, :LANES] / acc_b[:, LANES:]
        o_ref[:, p * LANES:(p + 1) * LANES] = jnp.where(lane < HEAD_DIM, oa, ob).astype(o_ref.dtype)


def _moba_prompt(qa, ka_t, va_t, *, batch, seq):
    nq = seq // Q_BLOCK
    nb = seq // MOBA_BLOCK
    kv_spec = pl.BlockSpec((None, W_MIX, seq), lambda b, c: (b, 0, 0))
    return pl.pallas_call(
        functools.partial(_moba_prompt_body, seq=seq),
        grid=(batch, nq),
        in_specs=[pl.BlockSpec((Q_BLOCK, W_MIX), lambda b, c: (b * nq + c, 0)), kv_spec, kv_spec],
        out_specs=pl.BlockSpec((Q_BLOCK, W_MIX), lambda b, c: (b * nq + c, 0)),
        out_shape=jax.ShapeDtypeStruct((batch * seq, W_MIX), BF16),
        scratch_shapes=[pltpu.VMEM((N_PAIRS, nb, 2 * LANES, MOBA_BLOCK), BF16),
                        pltpu.VMEM((N_PAIRS, nb, 2 * LANES, MOBA_BLOCK), BF16),
                        pltpu.VMEM((W_MIX, LANES), BF16),
                        pltpu.VMEM((W_MIX, LANES), BF16),
                        pltpu.VMEM((N_HEADS, Q_BLOCK, 2 * LANES), BF16),
                        pltpu.VMEM((N_HEADS, Q_BLOCK, LANES), F32),
                        pltpu.VMEM((N_HEADS, Q_BLOCK, 2 * LANES), F32)],
        compiler_params=pltpu.CompilerParams(
            dimension_semantics=("arbitrary", "arbitrary"), vmem_limit_bytes=VMEM_LIMIT),
        name="moba_prompt",
    )(qa, ka_t, va_t)


SB_TILE = 256


def _suffix_sum_matrix(n):
    r = lax.broadcasted_iota(jnp.int32, (n, n), 0)
    c = lax.broadcasted_iota(jnp.int32, (n, n), 1)
    return jnp.where(r > c, 1.0, 0.0).astype(BF16)


def _sb_log_terms(z2, past, tri):
    ls = _log2_sigmoid(z2)
    lk = ls - z2
    if past is not None:
        lk = jnp.where(past, lk, 0.0)
    tail = jnp.dot(lk.astype(BF16), tri, preferred_element_type=F32)
    return ls, lk, tail


def _sb_weights(z2, past, later, tri):
    ls, lk, tail = _sb_log_terms(z2, past, tri)
    a = jnp.exp2(ls + tail + later)
    if past is not None:
        a = jnp.where(past, a, 0.0)
    return a, jnp.sum(lk, axis=1, keepdims=True)


def _sb_prompt_body(q_ref, k_ref, v_ref, o_ref, kb, vb, tri, later_s, acc_s, *, seq):
    c = pl.program_id(1)

    @pl.when(c == 0)
    def _per_sequence_setup():
        for t in range(seq // SB_TILE):
            sl = slice(t * SB_TILE, (t + 1) * SB_TILE)
            for p in range(N_PAIRS):
                kb[p, t] = k_ref[p * LANES:(p + 1) * LANES, sl].astype(BF16)
                vb[p, t] = v_ref[p * LANES:(p + 1) * LANES, sl].astype(BF16)
        tri[...] = _suffix_sum_matrix(SB_TILE)

    qf = q_ref[...].astype(F32)
    lane = lax.broadcasted_iota(jnp.int32, (Q_BLOCK, LANES), 1)
    qm = []
    for h in range(N_HEADS):
        p, j = divmod(h, 2)
        in_head = (lane >= j * HEAD_DIM) & (lane < (j + 1) * HEAD_DIM)
        qm.append(jnp.where(in_head, qf[:, p * LANES:(p + 1) * LANES], 0.0).astype(BF16))
    later_s[...] = jnp.zeros(later_s.shape, F32)
    acc_s[...] = jnp.zeros(acc_s.shape, F32)

    diag = (c * Q_BLOCK) // SB_TILE
    qq = lax.broadcasted_iota(jnp.int32, (Q_BLOCK, SB_TILE), 0)
    kk = lax.broadcasted_iota(jnp.int32, (Q_BLOCK, SB_TILE), 1)
    past_diag = kk < qq + (c * Q_BLOCK - diag * SB_TILE)

    def tile_step(t, past):
        heads = range(N_HEADS)
        z = [jnp.dot(qm[h], kb[h // 2, t], preferred_element_type=F32) for h in heads]
        ls = [_log2_sigmoid(z[h]) for h in heads]
        lk = [ls[h] - z[h] for h in heads]
        if past is not None:
            lk = [jnp.where(past, x, 0.0) for x in lk]
        u = tri[...]
        tail = [jnp.dot(lk[h].astype(BF16), u, preferred_element_type=F32) for h in heads]
        a = [jnp.exp2(ls[h] + tail[h]) for h in heads]
        if past is not None:
            a = [jnp.where(past, x, 0.0) for x in a]
        for h in heads:
            pv = lax.dot_general(a[h].astype(BF16), vb[h // 2, t], NT_DIMS,
                                 preferred_element_type=F32)
            later = later_s[h]
            acc_s[h] = acc_s[h] + jnp.exp2(later) * pv
            later_s[h] = later + jnp.sum(lk[h], axis=1, keepdims=True)

    tile_step(diag, past_diag)

    def earlier_tile(i, carry):
        tile_step(diag - 1 - i, None)
        return carry

    lax.fori_loop(0, diag, earlier_tile, 0)
    for p in range(N_PAIRS):
        o_ref[:, p * LANES:(p + 1) * LANES] = jnp.where(
            lane < HEAD_DIM, acc_s[2 * p], acc_s[2 * p + 1]).astype(o_ref.dtype)


def _sb_prompt(qs, ks_t, vs_t, *, batch, seq):
    nq = seq // Q_BLOCK
    nt = seq // SB_TILE
    kv_spec = pl.BlockSpec((None, W_MIX, seq), lambda b, c: (b, 0, 0))
    return pl.pallas_call(
        functools.partial(_sb_prompt_body, seq=seq),
        grid=(batch, nq),
        in_specs=[pl.BlockSpec((Q_BLOCK, W_MIX), lambda b, c: (b * nq + c, 0)), kv_spec, kv_spec],
        out_specs=pl.BlockSpec((Q_BLOCK, W_MIX), lambda b, c: (b * nq + c, 0)),
        out_shape=jax.ShapeDtypeStruct((batch * seq, W_MIX), BF16),
        scratch_shapes=[pltpu.VMEM((N_PAIRS, nt, LANES, SB_TILE), BF16),
                        pltpu.VMEM((N_PAIRS, nt, LANES, SB_TILE), BF16),
                        pltpu.VMEM((SB_TILE, SB_TILE), BF16),
                        pltpu.VMEM((N_HEADS, Q_BLOCK, LANES), F32),
                        pltpu.VMEM((N_HEADS, Q_BLOCK, LANES), F32)],
        compiler_params=pltpu.CompilerParams(
            dimension_semantics=("arbitrary", "arbitrary"), vmem_limit_bytes=VMEM_LIMIT),
        name="sb_prompt",
    )(qs, ks_t, vs_t)


PAGES_PER_STEP = 8
STEP_KEYS = PAGES_PER_STEP * PAGE_SIZE


def _decode_attn_body(pt_ref, q_ref, kn_ref, vn_ref, *refs, mode, n_groups, dec_seq):
    del pt_ref
    npg = PAGES_PER_STEP
    k_pages = refs[:npg]
    v_pages = refs[npg:2 * npg]
    o_ref = refs[2 * npg]
    qbd, z_s, a_s, anew_s, norm_s, o_acc = refs[2 * npg + 1:]
    ph = pl.program_id(1)
    g = pl.program_id(2)
    rows = N_HEADS * dec_seq
    assert rows <= LANES and dec_seq <= 8

    row_i = lax.broadcasted_iota(jnp.int32, (rows, LANES), 0) % dec_seq
    lane = lax.broadcasted_iota(jnp.int32, (rows, LANES), 1)

    @pl.when((ph == 0) & (g == 0))
    def _build_block_diagonal_queries():
        qf = q_ref[...].astype(F32)
        qt = jnp.concatenate([qf] * N_HEADS, axis=0)
        r = lax.broadcasted_iota(jnp.int32, qt.shape, 0)
        l = lax.broadcasted_iota(jnp.int32, qt.shape, 1)
        qbd[...] = jnp.where((l // HEAD_DIM) == (r // dec_seq), qt, 0.0).astype(BF16)

    @pl.when(ph == 0)
    def _score_pages():
        keys = jnp.concatenate([k_pages[i][...].astype(BF16) for i in range(npg)], axis=1)
        z_s[g] = jnp.dot(qbd[...], keys, preferred_element_type=F32)

    def new_token_scores():
        kn = jnp.concatenate([kn_ref[...], jnp.zeros((LANES - dec_seq, W_MIX), F32)], axis=0)
        return lax.dot_general(qbd[...], kn.astype(BF16), NT_DIMS, preferred_element_type=F32)

    @pl.when((ph == 0) & (g == n_groups - 1))
    def _weights():
        zn = new_token_scores()
        if mode == "sb":
            tri = _suffix_sum_matrix(SB_TILE)
            past_new = lane < row_i
            a_new, later = _sb_weights(zn, past_new, 0.0, _suffix_sum_matrix(LANES))
            anew_s[...] = a_new.astype(BF16)
            for gg in range(n_groups - 1, -1, -1):
                for t in range(STEP_KEYS // SB_TILE - 1, -1, -1):
                    sl = slice(t * SB_TILE, (t + 1) * SB_TILE)
                    a, tot = _sb_weights(z_s[gg, :, sl], None, later, tri)
                    a_s[gg, :, sl] = a.astype(BF16)
                    later = later + tot
            norm_s[...] = jnp.ones(norm_s.shape, F32)
        else:
            blocks_per_step = STEP_KEYS // MOBA_BLOCK
            n_blocks = n_groups * blocks_per_step
            assert n_blocks <= LANES
            gate = jnp.zeros((rows, LANES), F32)
            for gg in range(n_groups):
                for t in range(blocks_per_step):
                    blk = z_s[gg, :, t * MOBA_BLOCK:(t + 1) * MOBA_BLOCK]
                    gate = jnp.where(lane == gg * blocks_per_step + t,
                                     jnp.sum(blk, axis=1, keepdims=True), gate)
            gm = jnp.where(lane < n_blocks, gate, -jnp.inf)
            sel = jnp.zeros((rows, LANES), F32)
            for _ in range(min(MOBA_TOPK, n_blocks)):
                mx = jnp.max(gm, axis=1, keepdims=True)
                idx = jnp.min(jnp.where(gm == mx, lane, LANES), axis=1, keepdims=True)
                pick = lane == idx
                sel = jnp.where(pick, 1.0, sel)
                gm = jnp.where(pick, -jnp.inf, gm)
            sel_b = sel.astype(BF16)
            own = (lane <= row_i) & (lane < dec_seq)
            zn = jnp.where(own, zn, NEG_BIG)
            mx = jnp.max(zn, axis=1, keepdims=True)
            masked = []
            for gg in range(n_groups):
                bl = lax.broadcasted_iota(jnp.int32, (LANES, STEP_KEYS), 0)
                bk = lax.broadcasted_iota(jnp.int32, (LANES, STEP_KEYS), 1)
                expand = jnp.where(bl == gg * blocks_per_step + bk // MOBA_BLOCK, 1.0, 0.0).astype(BF16)
                chosen = jnp.dot(sel_b, expand, preferred_element_type=F32) > 0.5
                s = jnp.where(chosen, z_s[gg], NEG_BIG)
                masked.append(s)
                mx = jnp.maximum(mx, jnp.max(s, axis=1, keepdims=True))
            p_new = jnp.exp2(zn - mx)
            denom = jnp.sum(p_new, axis=1, keepdims=True)
            anew_s[...] = p_new.astype(BF16)
            for gg in range(n_groups):
                pr = jnp.exp2(masked[gg] - mx)
                denom = denom + jnp.sum(pr, axis=1, keepdims=True)
                a_s[gg] = pr.astype(BF16)
            norm_s[...] = denom

    @pl.when((ph == 1) & (g == 0))
    def _new_token_values():
        vn = jnp.concatenate([vn_ref[...], jnp.zeros((LANES - dec_seq, W_MIX), F32)], axis=0)
        o_acc[...] = jnp.dot(anew_s[...], vn.astype(BF16), preferred_element_type=F32)

    @pl.when(ph == 1)
    def _weighted_values():
        vals = jnp.concatenate([v_pages[i][...].astype(BF16) for i in range(npg)], axis=1)
        o_acc[...] = o_acc[...] + lax.dot_general(a_s[g], vals, NT_DIMS, preferred_element_type=F32)

    @pl.when((ph == 1) & (g == n_groups - 1))
    def _write_out():
        o = o_acc[...] / norm_s[...]
        r = lax.broadcasted_iota(jnp.int32, (dec_seq, W_MIX), 1) // HEAD_DIM
        out = jnp.zeros((dec_seq, W_MIX), F32)
        for h in range(N_HEADS):
            out = jnp.where(r == h, o[h * dec_seq:(h + 1) * dec_seq, :], out)
        o_ref[...] = out.astype(o_ref.dtype)


def _decode_attn(page_table_flat, q, k_new, v_new, cache_k, cache_v, *, mode, n_pages):
    dec_batch, dec_seq, _ = q.shape
    npg = PAGES_PER_STEP
    n_groups = n_pages // npg
    rows = N_HEADS * dec_seq
    small = pl.BlockSpec((None, dec_seq, W_MIX), lambda b, ph, g, pt: (b, 0, 0))

    def k_page(i):
        def index(b, ph, g, pt):
            page = jnp.where(ph == 0, g * npg + i, n_pages - npg + i)
            return (pt[b * n_pages + page], 0, 0)
        return pl.BlockSpec((None, W_MIX, PAGE_SIZE), index)

    def v_page(i):
        def index(b, ph, g, pt):
            page = jnp.where(ph == 1, g * npg + i, i)
            return (pt[b * n_pages + page], 0, 0)
        return pl.BlockSpec((None, W_MIX, PAGE_SIZE), index)

    grid_spec = pltpu.PrefetchScalarGridSpec(
        num_scalar_prefetch=1,
        grid=(dec_batch, 2, n_groups),
        in_specs=[small, small, small] + [k_page(i) for i in range(npg)]
                 + [v_page(i) for i in range(npg)],
        out_specs=small,
        scratch_shapes=[pltpu.VMEM((rows, W_MIX), BF16),
                        pltpu.VMEM((n_groups, rows, STEP_KEYS), F32),
                        pltpu.VMEM((n_groups, rows, STEP_KEYS), BF16),
                        pltpu.VMEM((rows, LANES), BF16),
                        pltpu.VMEM((rows, 1), F32),
                        pltpu.VMEM((rows, W_MIX), F32)],
    )
    return pl.pallas_call(
        functools.partial(_decode_attn_body, mode=mode, n_groups=n_groups, dec_seq=dec_seq),
        grid_spec=grid_spec,
        out_shape=jax.ShapeDtypeStruct((dec_batch, dec_seq, W_MIX), BF16),
        compiler_params=pltpu.CompilerParams(
            dimension_semantics=("arbitrary", "arbitrary", "arbitrary"),
            vmem_limit_bytes=VMEM_LIMIT),
        name="decode_" + mode,
    )(page_table_flat, q, k_new, v_new, *([cache_k] * npg), *([cache_v] * npg))


def _rope_tables(pos):
    inv = ROPE_THETA ** (-jnp.arange(HALF_DIM, dtype=F32) / HALF_DIM)
    ang = pos.astype(F32)[:, None] * inv[None, :]
    cos, sin = jnp.cos(ang), jnp.sin(ang)
    cos_lane = jnp.tile(cos, (1, LANES // HALF_DIM))
    sin_lane = jnp.tile(jnp.concatenate([-sin, sin], axis=1), (1, LANES // HEAD_DIM))
    return cos_lane, sin_lane, cos.T, sin.T


def kernel(x_prompt, x_sample, cache_moba_k, cache_moba_v, cache_sb_k, cache_sb_v, page_table,
           g_mix, w_in, w_branch_moba, w_branch_sb, w_out, g_ffn, w_ffn_gate, w_ffn_up,
           w_ffn_down, g_final):
    batch, seq, d = x_prompt.shape
    dec_batch, dec_seq, _ = x_sample.shape
    depth = w_in.shape[0]
    n_pages = page_table.shape[1]
    past_len = n_pages * PAGE_SIZE
    n_pool = cache_moba_k.shape[1]
    assert seq % 512 == 0 and past_len % MOBA_BLOCK == 0 and n_pages % PAGES_PER_STEP == 0
    assert depth == 1

    cos_p, sin_p, cos_pt, sin_pt = _rope_tables(jnp.arange(seq, dtype=jnp.int32))
    cos_p, sin_p = jnp.tile(cos_p, (batch, 1)), jnp.tile(sin_p, (batch, 1))
    cos_s, sin_s, _, _ = _rope_tables(past_len + jnp.arange(dec_seq, dtype=jnp.int32))
    cos_s, sin_s = jnp.tile(cos_s, (dec_batch, 1)), jnp.tile(sin_s, (dec_batch, 1))
    pt_flat = page_table.reshape(-1)

    xp = x_prompt.reshape(batch * seq, d)
    xs = x_sample.reshape(dec_batch * dec_seq, d)
    m_s = dec_batch * dec_seq
    g_last = g_final.reshape(1, d)
    l = 0
    w_in_b = w_in[l].astype(BF16)
    wq = jnp.concatenate([w_in_b[:, 0:W_MIX], w_in_b[:, 3 * W_MIX:4 * W_MIX]], axis=1)
    wkv = jnp.concatenate([w_in_b[:, W_MIX:3 * W_MIX], w_in_b[:, 4 * W_MIX:6 * W_MIX]], axis=1)
    wgt = w_in_b[:, 6 * W_MIX:]
    bf = lambda w: w[l].astype(BF16)
    wa, ws, wo = bf(w_branch_moba), bf(w_branch_sb), bf(w_out)
    wg, wu, wd = bf(w_ffn_gate), bf(w_ffn_up), bf(w_ffn_down)
    gm, gf = g_mix[l].reshape(1, d), g_ffn[l].reshape(1, d)

    ka_t, va_t, ks_t, vs_t, qa, qs, ga, gs = _proj(
        xp, gm, wq, wkv.T, wgt, cos_p, sin_p, cos_pt, sin_pt, bm=512, seq=seq)
    oa = _moba_prompt(qa, ka_t, va_t, batch=batch, seq=seq)
    os_ = _sb_prompt(qs, ks_t, vs_t, batch=batch, seq=seq)
    y_prompt = _post(xp, oa, os_, ga, gs, wa, ws, wo, gf, wg, wu, wd, g_last, bm=512)

    ka2, va2, ks2, vs2, qa2, qs2, ga2, gs2 = _proj(xs, gm, wq, wkv, wgt, cos_s, sin_s, bm=m_s)
    r3 = lambda t: t.reshape(dec_batch, dec_seq, W_MIX)
    pool = lambda cch: cch[l].transpose(0, 2, 3, 1).reshape(n_pool, W_MIX, PAGE_SIZE)
    oa2 = _decode_attn(pt_flat, r3(qa2), r3(ka2), r3(va2), pool(cache_moba_k),
                       pool(cache_moba_v), mode="moba", n_pages=n_pages)
    os2 = _decode_attn(pt_flat, r3(qs2), r3(ks2), r3(vs2), pool(cache_sb_k),
                       pool(cache_sb_v), mode="sb", n_pages=n_pages)
    y_sample = _post(xs, oa2.reshape(m_s, W_MIX), os2.reshape(m_s, W_MIX), ga2, gs2,
                     wa, ws, wo, gf, wg, wu, wd, g_last, bm=m_s)

    rows_p = lambda t: t.reshape(1, batch, N_HEADS, HEAD_DIM, seq).transpose(0, 1, 4, 2, 3)
    rows_s = lambda t: t.reshape(1, dec_batch, dec_seq, N_HEADS, HEAD_DIM)
    return (y_prompt.reshape(batch, seq, d), y_sample.reshape(dec_batch, dec_seq, d),
            rows_p(ka_t), rows_p(va_t), rows_p(ks_t), rows_p(vs_t),
            rows_s(ka2), rows_s(va2), rows_s(ks2), rows_s(vs2))
```

```python
import functools
import math

import jax
import jax.numpy as jnp
from jax import lax
from jax.experimental import pallas as pl
from jax.experimental.pallas import tpu as pltpu

F32 = jnp.float32
BF16 = jnp.bfloat16

HEAD_DIM = 64
HALF_DIM = HEAD_DIM // 2
N_HEADS = 8
W_MIX = N_HEADS * HEAD_DIM
LANES = 128
N_PAIRS = W_MIX // LANES
MOBA_BLOCK = 256
MOBA_TOPK = 3
Q_BLOCK = 128
PAGE_SIZE = 128
ROPE_THETA = 10000.0
RMS_EPS = 1e-6
NEG_BIG = -1e30
SCORE_SCALE = math.log2(math.e) / math.sqrt(HEAD_DIM)
NT_DIMS = (((1,), (1,)), ((), ()))
VMEM_LIMIT = 56 * 1024 * 1024


def _resident(shape):
    return pl.BlockSpec(shape, lambda *_: (0,) * len(shape), pipeline_mode=pl.Buffered(1))


def _log2_sigmoid(z2):
    return jnp.minimum(z2, 0.0) - jnp.log2(1.0 + jnp.exp2(-jnp.abs(z2)))


def _split_bf16(x):
    hi = x.astype(BF16)
    lo = (x - hi.astype(F32)).astype(BF16)
    return hi, lo


def _rms_norm_bf16(x, g):
    ms = jnp.mean(x * x, axis=-1, keepdims=True)
    return (x * lax.rsqrt(ms + RMS_EPS) * g).astype(BF16)


def _proj_body(x_ref, g_ref, wq_ref, wkv_ref, wg_ref, cos_ref, sin_ref, *refs, kv_transposed):
    if kv_transposed:
        cos_t_ref, sin_t_ref = refs[:2]
        refs = refs[2:]
    ka_ref, va_ref, ks_ref, vs_ref, qa_ref, qs_ref, ga_ref, gs_ref = refs
    h = _rms_norm_bf16(x_ref[...], g_ref[...])
    d_model = ga_ref.shape[1]

    cos = cos_ref[...]
    sin = sin_ref[...]
    lane = lax.broadcasted_iota(jnp.int32, cos.shape, 1)
    first_half = (lane % HEAD_DIM) < HALF_DIM

    def rope_slab(xs):
        rot = jnp.where(first_half, pltpu.roll(xs, LANES - HALF_DIM, 1), pltpu.roll(xs, HALF_DIM, 1))
        return xs * cos + rot * sin

    q = jnp.dot(h, wq_ref[...], preferred_element_type=F32)
    for p in range(N_PAIRS):
        sl = slice(p * LANES, (p + 1) * LANES)
        qa_ref[:, sl] = (rope_slab(q[:, sl]) * SCORE_SCALE).astype(qa_ref.dtype)
    qs_ref[...] = (q[:, W_MIX:] * SCORE_SCALE).astype(qs_ref.dtype)

    if kv_transposed:
        def seg_t(i):
            return lax.dot_general(wkv_ref[i * W_MIX:(i + 1) * W_MIX, :], h, NT_DIMS,
                                   preferred_element_type=F32)
        ka = seg_t(0)
        cos_t = cos_t_ref[...]
        sin_t = sin_t_ref[...]
        for hd in range(N_HEADS):
            x1 = ka[hd * HEAD_DIM:hd * HEAD_DIM + HALF_DIM, :]
            x2 = ka[hd * HEAD_DIM + HALF_DIM:(hd + 1) * HEAD_DIM, :]
            ka_ref[hd * HEAD_DIM:hd * HEAD_DIM + HALF_DIM, :] = x1 * cos_t - x2 * sin_t
            ka_ref[hd * HEAD_DIM + HALF_DIM:(hd + 1) * HEAD_DIM, :] = x2 * cos_t + x1 * sin_t
        va_ref[...] = seg_t(1)
        ks_ref[...] = seg_t(2)
        vs_ref[...] = seg_t(3)
    else:
        def seg(i):
            return jnp.dot(h, wkv_ref[:, i * W_MIX:(i + 1) * W_MIX], preferred_element_type=F32)
        ka = seg(0)
        for p in range(N_PAIRS):
            sl = slice(p * LANES, (p + 1) * LANES)
            ka_ref[:, sl] = rope_slab(ka[:, sl])
        va_ref[...] = seg(1)
        ks_ref[...] = seg(2)
        vs_ref[...] = seg(3)

    gates = jnp.dot(h, wg_ref[...], preferred_element_type=F32)
    ga_ref[...] = jax.nn.sigmoid(gates[:, :d_model]).astype(ga_ref.dtype)
    gs_ref[...] = jax.nn.sigmoid(gates[:, d_model:]).astype(gs_ref.dtype)


def _proj(x, g_mix, wq, wkv, wg, cos, sin, cos_t=None, sin_t=None, *, bm, seq=None):
    m, d = x.shape
    kv_transposed = seq is not None
    row = lambda width: pl.BlockSpec((bm, width), lambda i: (i, 0))
    in_specs = [row(d), _resident((1, d)), _resident(wq.shape), _resident(wkv.shape),
                _resident(wg.shape), row(LANES), row(LANES)]
    args = [x, g_mix, wq, wkv, wg, cos, sin]
    if kv_transposed:
        per_seq = seq // bm
        tab = pl.BlockSpec((HALF_DIM, bm), lambda i: (0, i % per_seq))
        in_specs += [tab, tab]
        args += [cos_t, sin_t]
        kv_shape = jax.ShapeDtypeStruct((m // seq, W_MIX, seq), F32)
        kv_spec = pl.BlockSpec((None, W_MIX, bm), lambda i: (i // per_seq, 0, i % per_seq))
    else:
        kv_shape = jax.ShapeDtypeStruct((m, W_MIX), F32)
        kv_spec = row(W_MIX)
    return pl.pallas_call(
        functools.partial(_proj_body, kv_transposed=kv_transposed),
        grid=(m // bm,),
        in_specs=in_specs,
        out_specs=[kv_spec] * 4 + [row(W_MIX)] * 2 + [row(d)] * 2,
        out_shape=[kv_shape] * 4 + [jax.ShapeDtypeStruct((m, W_MIX), BF16)] * 2
                  + [jax.ShapeDtypeStruct((m, d), BF16)] * 2,
        compiler_params=pltpu.CompilerParams(
            dimension_semantics=("arbitrary",), vmem_limit_bytes=VMEM_LIMIT),
        name="proj_prompt" if kv_transposed else "proj_decode",
    )(*args)


def _post_body(x_ref, oa_ref, os_ref, ga_ref, gs_ref, wa_ref, ws_ref, wo_ref,
               gf_ref, wg_ref, wu_ref, wd_ref, gfin_ref, y_ref, *, ff_chunk):
    merged = (ga_ref[...].astype(F32) * jnp.dot(oa_ref[...], wa_ref[...], preferred_element_type=F32)
              + gs_ref[...].astype(F32) * jnp.dot(os_ref[...], ws_ref[...], preferred_element_type=F32))
    x1 = x_ref[...] + jnp.dot(merged.astype(BF16), wo_ref[...], preferred_element_type=F32)
    h2 = _rms_norm_bf16(x1, gf_ref[...])
    d_ff = wg_ref.shape[1]
    x2 = x1
    for lo in range(0, d_ff, ff_chunk):
        gate = jnp.dot(h2, wg_ref[:, lo:lo + ff_chunk], preferred_element_type=F32)
        up = jnp.dot(h2, wu_ref[:, lo:lo + ff_chunk], preferred_element_type=F32)
        ff = (gate * jax.nn.sigmoid(gate) * up).astype(BF16)
        x2 = x2 + jnp.dot(ff, wd_ref[lo:lo + ff_chunk, :], preferred_element_type=F32)
    ms2 = jnp.mean(x2 * x2, axis=-1, keepdims=True)
    y_ref[...] = x2 * lax.rsqrt(ms2 + RMS_EPS) * gfin_ref[...]


def _post(x, oa, os_, ga, gs, wa, ws, wo, g_ffn, wg, wu, wd, g_final, *, bm):
    m, d = x.shape
    d_ff = wg.shape[1]
    row = lambda width: pl.BlockSpec((bm, width), lambda i: (i, 0))
    return pl.pallas_call(
        functools.partial(_post_body, ff_chunk=d_ff // 2),
        grid=(m // bm,),
        in_specs=[row(d), row(W_MIX), row(W_MIX), row(d), row(d),
                  _resident((W_MIX, d)), _resident((W_MIX, d)), _resident((d, d)),
                  _resident((1, d)), _resident((d, d_ff)), _resident((d, d_ff)),
                  _resident((d_ff, d)), _resident((1, d))],
        out_specs=row(d),
        out_shape=jax.ShapeDtypeStruct((m, d), F32),
        compiler_params=pltpu.CompilerParams(
            dimension_semantics=("arbitrary",), vmem_limit_bytes=VMEM_LIMIT),
        name="post_ffn",
    )(x, oa, os_, ga, gs, wa, ws, wo, g_ffn, wg, wu, wd, g_final)


GATE_GROUP = 8


def _moba_prompt_body(q_ref, k_ref, v_ref, o_ref,
                      kaug, vb, kmhi, kmlo, lhs, m_s, acc_s, *, seq):
    c = pl.program_id(1)
    nb = seq // MOBA_BLOCK
    assert nb <= GATE_GROUP and N_HEADS * GATE_GROUP <= LANES

    @pl.when(c == 0)
    def _per_sequence_setup():
        r = lax.broadcasted_iota(jnp.int32, (LANES, MOBA_BLOCK), 0)
        lane_w = lax.broadcasted_iota(jnp.int32, (W_MIX, LANES), 1)
        row_w = lax.broadcasted_iota(jnp.int32, (W_MIX, LANES), 0)
        km = jnp.zeros((W_MIX, LANES), F32)
        for t in range(nb):
            sl = slice(t * MOBA_BLOCK, (t + 1) * MOBA_BLOCK)
            ind = jnp.where((r < N_HEADS * GATE_GROUP) & ((r % GATE_GROUP) == t), 1.0, 0.0).astype(BF16)
            for p in range(N_PAIRS):
                kaug[p, t, 0:LANES, :] = k_ref[p * LANES:(p + 1) * LANES, sl].astype(BF16)
                kaug[p, t, LANES:2 * LANES, :] = ind
                vb[p, t, 0:LANES, :] = v_ref[p * LANES:(p + 1) * LANES, sl].astype(BF16)
                vb[p, t, LANES:2 * LANES, :] = jnp.ones((LANES, MOBA_BLOCK), BF16)
            block_sum = jnp.sum(k_ref[:, sl], axis=1, keepdims=True)
            km = jnp.where((lane_w % GATE_GROUP) == t, block_sum, km)
        km = jnp.where((lane_w < N_HEADS * GATE_GROUP) & ((row_w // HEAD_DIM) == (lane_w // GATE_GROUP)),
                       km * (1.0 / MOBA_BLOCK), 0.0)
        hi, lo = _split_bf16(km)
        kmhi[...] = hi
        kmlo[...] = lo

    q = q_ref[...]
    gate = (jnp.dot(q, kmhi[...], preferred_element_type=F32)
            + jnp.dot(q, kmlo[...], preferred_element_type=F32))
    lane = lax.broadcasted_iota(jnp.int32, (Q_BLOCK, LANES), 1)
    n = lane % GATE_GROUP
    cur = (c * Q_BLOCK) // MOBA_BLOCK
    g = jnp.where(n < cur, gate, -jnp.inf)
    rank = jnp.zeros((Q_BLOCK, LANES), F32)
    for r in range(1, GATE_GROUP):
        wraps = (n + r) >= GATE_GROUP
        other = jnp.where(wraps, pltpu.roll(g, GATE_GROUP - r, 1), pltpu.roll(g, LANES - r, 1))
        beats = (other > g) | (wraps & (other == g))
        rank = rank + beats.astype(F32)
    keep = ((n < cur) & (rank < MOBA_TOPK)) | (n >= cur)
    bias = jnp.where(keep | (lane >= N_HEADS * GATE_GROUP), 0.0, NEG_BIG)

    qf = q.astype(F32)
    for h in range(N_HEADS):
        p, j = divmod(h, 2)
        in_head = (lane >= j * HEAD_DIM) & (lane < (j + 1) * HEAD_DIM)
        lhs[h, :, 0:LANES] = jnp.where(in_head, qf[:, p * LANES:(p + 1) * LANES], 0.0).astype(BF16)
        in_group = (lane >= h * GATE_GROUP) & (lane < (h + 1) * GATE_GROUP)
        lhs[h, :, LANES:2 * LANES] = jnp.where(in_group, bias, 0.0).astype(BF16)
    m_s[...] = jnp.full(m_s.shape, NEG_BIG, F32)
    acc_s[...] = jnp.zeros(acc_s.shape, F32)

    qq = lax.broadcasted_iota(jnp.int32, (Q_BLOCK, MOBA_BLOCK), 0)
    kk = lax.broadcasted_iota(jnp.int32, (Q_BLOCK, MOBA_BLOCK), 1)
    causal = kk <= qq + (c * Q_BLOCK - cur * MOBA_BLOCK)
    twice = lambda x: jnp.concatenate([x, x], axis=1)

    def tile_step(t, own):
        heads = range(N_HEADS)
        s = [jnp.dot(lhs[h], kaug[h // 2, t], preferred_element_type=F32) for h in heads]
        if own:
            s = [jnp.where(causal, x, NEG_BIG) for x in s]
        m_old = [m_s[h] for h in heads]
        m_new = [jnp.maximum(m_old[h], jnp.max(s[h], axis=1, keepdims=True)) for h in heads]
        alpha = [jnp.exp2(m_old[h] - m_new[h]) for h in heads]
        pr = [jnp.exp2(s[h] - twice(m_new[h])).astype(BF16) for h in heads]
        for h in heads:
            pv = lax.dot_general(pr[h], vb[h // 2, t], NT_DIMS, preferred_element_type=F32)
            acc_s[h] = twice(alpha[h]) * acc_s[h] + pv
            m_s[h] = m_new[h]

    def past_tile(t, carry):
        tile_step(t, own=False)
        return carry

    lax.fori_loop(0, cur, past_tile, 0)
    tile_step(cur, own=True)

    for p in range(N_PAIRS):
        acc_a, acc_b = acc_s[2 * p], acc_s[2 * p + 1]
        oa = acc_a[:, :LANES] / acc_a[:, LANES:]
        ob = acc_b[:, :LANES] / acc_b[:, LANES:]
        o_ref[:, p * LANES:(p + 1) * LANES] = jnp.where(lane < HEAD_DIM, oa, ob).astype(o_ref.dtype)


def _moba_prompt(qa, ka_t, va_t, *, batch, seq):
    nq = seq // Q_BLOCK
    nb = seq // MOBA_BLOCK
    kv_spec = pl.BlockSpec((None, W_MIX, seq), lambda b, c: (b, 0, 0))
    return pl.pallas_call(
        functools.partial(_moba_prompt_body, seq=seq),
        grid=(batch, nq),
        in_specs=[pl.BlockSpec((Q_BLOCK, W_MIX), lambda b, c: (b * nq + c, 0)), kv_spec, kv_spec],
        out_specs=pl.BlockSpec((Q_BLOCK, W_MIX), lambda b, c: (b * nq + c, 0)),
        out_shape=jax.ShapeDtypeStruct((batch * seq, W_MIX), BF16),
        scratch_shapes=[pltpu.VMEM((N_PAIRS, nb, 2 * LANES, MOBA_BLOCK), BF16),
                        pltpu.VMEM((N_PAIRS, nb, 2 * LANES, MOBA_BLOCK), BF16),
                        pltpu.VMEM((W_MIX, LANES), BF16),
                        pltpu.VMEM((W_MIX, LANES), BF16),
                        pltpu.VMEM((N_HEADS, Q_BLOCK, 2 * LANES), BF16),
                        pltpu.VMEM((N_HEADS, Q_BLOCK, LANES), F32),
                        pltpu.VMEM((N_HEADS, Q_BLOCK, 2 * LANES), F32)],
        compiler_params=pltpu.CompilerParams(
            dimension_semantics=("arbitrary", "arbitrary"), vmem_limit_bytes=VMEM_LIMIT),
        name="moba_prompt",
    )(qa, ka_t, va_t)


SB_TILE = 256


def _suffix_sum_matrix(n):
    r = lax.broadcasted_iota(jnp.int32, (n, n), 0)
    c = lax.broadcasted_iota(jnp.int32, (n, n), 1)
    return jnp.where(r > c, 1.0, 0.0).astype(BF16)


def _sb_log_terms(z2, past, tri):
    ls = _log2_sigmoid(z2)
    lk = ls - z2
    if past is not None:
        lk = jnp.where(past, lk, 0.0)
    tail = jnp.dot(lk.astype(BF16), tri, preferred_element_type=F32)
    return ls, lk, tail


def _sb_weights(z2, past, later, tri):
    ls, lk, tail = _sb_log_terms(z2, past, tri)
    a = jnp.exp2(ls + tail + later)
    if past is not None:
        a = jnp.where(past, a, 0.0)
    return a, jnp.sum(lk, axis=1, keepdims=True)


def _sb_prompt_body(q_ref, k_ref, v_ref, o_ref, kb, vb, tri, later_s, acc_s, *, seq):
    c = pl.program_id(1)

    @pl.when(c == 0)
    def _per_sequence_setup():
        for t in range(seq // SB_TILE):
            sl = slice(t * SB_TILE, (t + 1) * SB_TILE)
            for p in range(N_PAIRS):
                kb[p, t] = k_ref[p * LANES:(p + 1) * LANES, sl].astype(BF16)
                vb[p, t] = v_ref[p * LANES:(p + 1) * LANES, sl].astype(BF16)
        tri[...] = _suffix_sum_matrix(SB_TILE)

    qf = q_ref[...].astype(F32)
    lane = lax.broadcasted_iota(jnp.int32, (Q_BLOCK, LANES), 1)
    qm = []
    for h in range(N_HEADS):
        p, j = divmod(h, 2)
        in_head = (lane >= j * HEAD_DIM) & (lane < (j + 1) * HEAD_DIM)
        qm.append(jnp.where(in_head, qf[:, p * LANES:(p + 1) * LANES], 0.0).astype(BF16))
    later_s[...] = jnp.zeros(later_s.shape, F32)
    acc_s[...] = jnp.zeros(acc_s.shape, F32)

    diag = (c * Q_BLOCK) // SB_TILE
    qq = lax.broadcasted_iota(jnp.int32, (Q_BLOCK, SB_TILE), 0)
    kk = lax.broadcasted_iota(jnp.int32, (Q_BLOCK, SB_TILE), 1)
    past_diag = kk < qq + (c * Q_BLOCK - diag * SB_TILE)

    def tile_step(t, past):
        heads = range(N_HEADS)
        z = [jnp.dot(qm[h], kb[h // 2, t], preferred_element_type=F32) for h in heads]
        ls = [_log2_sigmoid(z[h]) for h in heads]
        lk = [ls[h] - z[h] for h in heads]
        if past is not None:
            lk = [jnp.where(past, x, 0.0) for x in lk]
        u = tri[...]
        tail = [jnp.dot(lk[h].astype(BF16), u, preferred_element_type=F32) for h in heads]
        a = [jnp.exp2(ls[h] + tail[h]) for h in heads]
        if past is not None:
            a = [jnp.where(past, x, 0.0) for x in a]
        for h in heads:
            pv = lax.dot_general(a[h].astype(BF16), vb[h // 2, t], NT_DIMS,
                                 preferred_element_type=F32)
            later = later_s[h]
            acc_s[h] = acc_s[h] + jnp.exp2(later) * pv
            later_s[h] = later + jnp.sum(lk[h], axis=1, keepdims=True)

    tile_step(diag, past_diag)

    def earlier_tile(i, carry):
        tile_step(diag - 1 - i, None)
        return carry

    lax.fori_loop(0, diag, earlier_tile, 0)
    for p in range(N_PAIRS):
        o_ref[:, p * LANES:(p + 1) * LANES] = jnp.where(
            lane < HEAD_DIM, acc_s[2 * p], acc_s[2 * p + 1]).astype(o_ref.dtype)


def _sb_prompt(qs, ks_t, vs_t, *, batch, seq):
    nq = seq // Q_BLOCK
    nt = seq // SB_TILE
    kv_spec = pl.BlockSpec((None, W_MIX, seq), lambda b, c: (b, 0, 0))
    return pl.pallas_call(
        functools.partial(_sb_prompt_body, seq=seq),
        grid=(batch, nq),
        in_specs=[pl.BlockSpec((Q_BLOCK, W_MIX), lambda b, c: (b * nq + c, 0)), kv_spec, kv_spec],
        out_specs=pl.BlockSpec((Q_BLOCK, W_MIX), lambda b, c: (b * nq + c, 0)),
        out_shape=jax.ShapeDtypeStruct((batch * seq, W_MIX), BF16),
        scratch_shapes=[pltpu.VMEM((N_PAIRS, nt, LANES, SB_TILE), BF16),
                        pltpu.VMEM((N_PAIRS, nt, LANES, SB_TILE), BF16),
                        pltpu.VMEM((SB_TILE, SB_TILE), BF16),
                        pltpu.VMEM((N_HEADS, Q_BLOCK, LANES), F32),
                        pltpu.VMEM((N_HEADS, Q_BLOCK, LANES), F32)],
        compiler_params=pltpu.CompilerParams(
            dimension_semantics=("arbitrary", "arbitrary"), vmem_limit_bytes=VMEM_LIMIT),
        name="sb_prompt",
    )(qs, ks_t, vs_t)


PAGES_PER_STEP = 8
STEP_KEYS = PAGES_PER_STEP * PAGE_SIZE
RING_STEPS = 4


def _decode_attn_body(pt_ref, q_ref, kn_ref, vn_ref, ck_hbm, cv_hbm, o_ref,
                      pages, sems, qbd, z_s, a_s, anew_s, norm_s, o_acc,
                      *, mode, n_pages, dec_seq, n_batch):
    npg = PAGES_PER_STEP
    n_groups = n_pages // npg
    n_steps = 2 * n_groups
    lookahead = RING_STEPS - 1
    assert n_steps % RING_STEPS == 0
    b = pl.program_id(0)
    rows = N_HEADS * dec_seq
    assert rows <= LANES and dec_seq <= 8

    def page_copy(step, j, batch):
        src = ck_hbm if step < n_groups else cv_hbm
        page = (step % n_groups) * npg + j
        ring = step % RING_STEPS
        return pltpu.make_async_copy(src.at[pt_ref[batch * n_pages + page]],
                                     pages.at[ring * npg + j], sems.at[ring])

    def start_step(step, batch):
        for j in range(npg):
            page_copy(step, j, batch).start()

    def wait_step(step):
        for j in range(npg):
            page_copy(step, j, b).wait()

    def step_pages(step):
        ring = step % RING_STEPS
        return jnp.concatenate([pages[ring * npg + j].astype(BF16) for j in range(npg)], axis=1)

    row_i = lax.broadcasted_iota(jnp.int32, (rows, LANES), 0) % dec_seq
    lane = lax.broadcasted_iota(jnp.int32, (rows, LANES), 1)

    @pl.when(b == 0)
    def _prime_ring():
        for s in range(lookahead):
            start_step(s, 0)

    qf = q_ref[...].astype(F32)
    qt = jnp.concatenate([qf] * N_HEADS, axis=0)
    r = lax.broadcasted_iota(jnp.int32, qt.shape, 0)
    l = lax.broadcasted_iota(jnp.int32, qt.shape, 1)
    qbd[...] = jnp.where((l // HEAD_DIM) == (r // dec_seq), qt, 0.0).astype(BF16)

    def weights():
        kn = jnp.concatenate([kn_ref[...], jnp.zeros((LANES - dec_seq, W_MIX), F32)], axis=0)
        zn = lax.dot_general(qbd[...], kn.astype(BF16), NT_DIMS, preferred_element_type=F32)
        if mode == "sb":
            tri = _suffix_sum_matrix(SB_TILE)
            past_new = lane < row_i
            a_new, later = _sb_weights(zn, past_new, 0.0, _suffix_sum_matrix(LANES))
            anew_s[...] = a_new.astype(BF16)
            for gg in range(n_groups - 1, -1, -1):
                for t in range(STEP_KEYS // SB_TILE - 1, -1, -1):
                    sl = slice(t * SB_TILE, (t + 1) * SB_TILE)
                    a, tot = _sb_weights(z_s[gg, :, sl], None, later, tri)
                    a_s[gg, :, sl] = a.astype(BF16)
                    later = later + tot
            norm_s[...] = jnp.ones(norm_s.shape, F32)
        else:
            blocks_per_step = STEP_KEYS // MOBA_BLOCK
            n_blocks = n_groups * blocks_per_step
            assert n_blocks <= LANES
            gate = jnp.zeros((rows, LANES), F32)
            for gg in range(n_groups):
                for t in range(blocks_per_step):
                    blk = z_s[gg, :, t * MOBA_BLOCK:(t + 1) * MOBA_BLOCK]
                    gate = jnp.where(lane == gg * blocks_per_step + t,
                                     jnp.sum(blk, axis=1, keepdims=True), gate)
            gm = jnp.where(lane < n_blocks, gate, -jnp.inf)
            sel = jnp.zeros((rows, LANES), F32)
            for _ in range(min(MOBA_TOPK, n_blocks)):
                mx = jnp.max(gm, axis=1, keepdims=True)
                idx = jnp.min(jnp.where(gm == mx, lane, LANES), axis=1, keepdims=True)
                pick = lane == idx
                sel = jnp.where(pick, 1.0, sel)
                gm = jnp.where(pick, -jnp.inf, gm)
            sel_b = sel.astype(BF16)
            own = (lane <= row_i) & (lane < dec_seq)
            zn = jnp.where(own, zn, NEG_BIG)
            mx = jnp.max(zn, axis=1, keepdims=True)
            masked = []
            for gg in range(n_groups):
                bl = lax.broadcasted_iota(jnp.int32, (LANES, STEP_KEYS), 0)
                bk = lax.broadcasted_iota(jnp.int32, (LANES, STEP_KEYS), 1)
                expand = jnp.where(bl == gg * blocks_per_step + bk // MOBA_BLOCK, 1.0, 0.0).astype(BF16)
                chosen = jnp.dot(sel_b, expand, preferred_element_type=F32) > 0.5
                s = jnp.where(chosen, z_s[gg], NEG_BIG)
                masked.append(s)
                mx = jnp.maximum(mx, jnp.max(s, axis=1, keepdims=True))
            p_new = jnp.exp2(zn - mx)
            denom = jnp.sum(p_new, axis=1, keepdims=True)
            anew_s[...] = p_new.astype(BF16)
            for gg in range(n_groups):
                pr = jnp.exp2(masked[gg] - mx)
                denom = denom + jnp.sum(pr, axis=1, keepdims=True)
                a_s[gg] = pr.astype(BF16)
            norm_s[...] = denom

    for step in range(n_steps):
        ahead = step + lookahead
        if ahead < n_steps:
            start_step(ahead, b)
        else:
            @pl.when(b + 1 < n_batch)
            def _start_next_sequence(ahead=ahead):
                start_step(ahead - n_steps, b + 1)
        wait_step(step)
        if step < n_groups:
            z_s[step] = jnp.dot(qbd[...], step_pages(step), preferred_element_type=F32)
            if step == n_groups - 1:
                weights()
                vn = jnp.concatenate([vn_ref[...], jnp.zeros((LANES - dec_seq, W_MIX), F32)], axis=0)
                o_acc[...] = jnp.dot(anew_s[...], vn.astype(BF16), preferred_element_type=F32)
        else:
            o_acc[...] = o_acc[...] + lax.dot_general(
                a_s[step - n_groups], step_pages(step), NT_DIMS, preferred_element_type=F32)

    o = o_acc[...] / norm_s[...]
    hd = lax.broadcasted_iota(jnp.int32, (dec_seq, W_MIX), 1) // HEAD_DIM
    out = jnp.zeros((dec_seq, W_MIX), F32)
    for h in range(N_HEADS):
        out = jnp.where(hd == h, o[h * dec_seq:(h + 1) * dec_seq, :], out)
    o_ref[...] = out.astype(o_ref.dtype)


def _decode_attn(page_table_flat, q, k_new, v_new, cache_k, cache_v, *, mode, n_pages):
    dec_batch, dec_seq, _ = q.shape
    npg = PAGES_PER_STEP
    n_groups = n_pages // npg
    rows = N_HEADS * dec_seq
    small = pl.BlockSpec((None, dec_seq, W_MIX), lambda b, pt: (b, 0, 0))
    hbm = pl.BlockSpec(memory_space=pl.ANY)
    grid_spec = pltpu.PrefetchScalarGridSpec(
        num_scalar_prefetch=1,
        grid=(dec_batch,),
        in_specs=[small, small, small, hbm, hbm],
        out_specs=small,
        scratch_shapes=[pltpu.VMEM((RING_STEPS * npg, W_MIX, PAGE_SIZE), F32),
                        pltpu.SemaphoreType.DMA((RING_STEPS,)),
                        pltpu.VMEM((rows, W_MIX), BF16),
                        pltpu.VMEM((n_groups, rows, STEP_KEYS), F32),
                        pltpu.VMEM((n_groups, rows, STEP_KEYS), BF16),
                        pltpu.VMEM((rows, LANES), BF16),
                        pltpu.VMEM((rows, 1), F32),
                        pltpu.VMEM((rows, W_MIX), F32)],
    )
    return pl.pallas_call(
        functools.partial(_decode_attn_body, mode=mode, n_pages=n_pages, dec_seq=dec_seq,
                          n_batch=dec_batch),
        grid_spec=grid_spec,
        out_shape=jax.ShapeDtypeStruct((dec_batch, dec_seq, W_MIX), BF16),
        compiler_params=pltpu.CompilerParams(
            dimension_semantics=("arbitrary",), vmem_limit_bytes=VMEM_LIMIT),
        name="decode_" + mode,
    )(page_table_flat, q, k_new, v_new, cache_k, cache_v)


def _rope_tables(pos):
    inv = ROPE_THETA ** (-jnp.arange(HALF_DIM, dtype=F32) / HALF_DIM)
    ang = pos.astype(F32)[:, None] * inv[None, :]
    cos, sin = jnp.cos(ang), jnp.sin(ang)
    cos_lane = jnp.tile(cos, (1, LANES // HALF_DIM))
    sin_lane = jnp.tile(jnp.concatenate([-sin, sin], axis=1), (1, LANES // HEAD_DIM))
    return cos_lane, sin_lane, cos.T, sin.T


def kernel(x_prompt, x_sample, cache_moba_k, cache_moba_v, cache_sb_k, cache_sb_v, page_table,
           g_mix, w_in, w_branch_moba, w_branch_sb, w_out, g_ffn, w_ffn_gate, w_ffn_up,
           w_ffn_down, g_final):
    batch, seq, d = x_prompt.shape
    dec_batch, dec_seq, _ = x_sample.shape
    depth = w_in.shape[0]
    n_pages = page_table.shape[1]
    past_len = n_pages * PAGE_SIZE
    n_pool = cache_moba_k.shape[1]
    assert seq % 512 == 0 and past_len % MOBA_BLOCK == 0 and n_pages % PAGES_PER_STEP == 0
    assert depth == 1

    cos_p, sin_p, cos_pt, sin_pt = _rope_tables(jnp.arange(seq, dtype=jnp.int32))
    cos_p, sin_p = jnp.tile(cos_p, (batch, 1)), jnp.tile(sin_p, (batch, 1))
    cos_s, sin_s, _, _ = _rope_tables(past_len + jnp.arange(dec_seq, dtype=jnp.int32))
    cos_s, sin_s = jnp.tile(cos_s, (dec_batch, 1)), jnp.tile(sin_s, (dec_batch, 1))
    pt_flat = page_table.reshape(-1)

    xp = x_prompt.reshape(batch * seq, d)
    xs = x_sample.reshape(dec_batch * dec_seq, d)
    m_s = dec_batch * dec_seq
    g_last = g_final.reshape(1, d)
    l = 0
    w_in_b = w_in[l].astype(BF16)
    wq = jnp.concatenate([w_in_b[:, 0:W_MIX], w_in_b[:, 3 * W_MIX:4 * W_MIX]], axis=1)
    wkv = jnp.concatenate([w_in_b[:, W_MIX:3 * W_MIX], w_in_b[:, 4 * W_MIX:6 * W_MIX]], axis=1)
    wgt = w_in_b[:, 6 * W_MIX:]
    bf = lambda w: w[l].astype(BF16)
    wa, ws, wo = bf(w_branch_moba), bf(w_branch_sb), bf(w_out)
    wg, wu, wd = bf(w_ffn_gate), bf(w_ffn_up), bf(w_ffn_down)
    gm, gf = g_mix[l].reshape(1, d), g_ffn[l].reshape(1, d)

    ka_t, va_t, ks_t, vs_t, qa, qs, ga, gs = _proj(
        xp, gm, wq, wkv.T, wgt, cos_p, sin_p, cos_pt, sin_pt, bm=512, seq=seq)
    oa = _moba_prompt(qa, ka_t, va_t, batch=batch, seq=seq)
    os_ = _sb_prompt(qs, ks_t, vs_t, batch=batch, seq=seq)
    y_prompt = _post(xp, oa, os_, ga, gs, wa, ws, wo, gf, wg, wu, wd, g_last, bm=512)

    ka2, va2, ks2, vs2, qa2, qs2, ga2, gs2 = _proj(xs, gm, wq, wkv, wgt, cos_s, sin_s, bm=m_s)
    r3 = lambda t: t.reshape(dec_batch, dec_seq, W_MIX)
    pool = lambda cch: cch[l].transpose(0, 2, 3, 1).reshape(n_pool, W_MIX, PAGE_SIZE)
    oa2 = _decode_attn(pt_flat, r3(qa2), r3(ka2), r3(va2), pool(cache_moba_k),
                       pool(cache_moba_v), mode="moba", n_pages=n_pages)
    os2 = _decode_attn(pt_flat, r3(qs2), r3(ks2), r3(vs2), pool(cache_sb_k),
                       pool(cache_sb_v), mode="sb", n_pages=n_pages)
    y_sample = _post(xs, oa2.reshape(m_s, W_MIX), os2.reshape(m_s, W_MIX), ga2, gs2,
                     wa, ws, wo, gf, wg, wu, wd, g_last, bm=m_s)

    rows_p = lambda t: t.reshape(1, batch, N_HEADS, HEAD_DIM, seq).transpose(0, 1, 4, 2, 3)
    rows_s = lambda t: t.reshape(1, dec_batch, dec_seq, N_HEADS, HEAD_DIM)
    return (y_prompt.reshape(batch, seq, d), y_sample.reshape(dec_batch, dec_seq, d),
            rows_p(ka_t), rows_p(va_t), rows_p(ks_t), rows_p(vs_t),
            rows_s(ka2), rows_s(va2), rows_s(ks2), rows_s(vs2))
```

```python
import functools
import math
from typing import NamedTuple

import jax
import jax.numpy as jnp
from jax import lax
from jax.experimental import pallas as pl
from jax.experimental.pallas import tpu as pltpu

F32 = jnp.float32
BF16 = jnp.bfloat16

HEAD_DIM = 64
HALF_DIM = HEAD_DIM // 2
N_HEADS = 8
W_MIX = N_HEADS * HEAD_DIM
LANES = 128
N_PAIRS = W_MIX // LANES
MOBA_BLOCK = 256
MOBA_TOPK = 3
Q_BLOCK = 128
PAGE_SIZE = 128
ROPE_THETA = 10000.0
RMS_EPS = 1e-6
NEG_BIG = -1e30
SCORE_SCALE = math.log2(math.e) / math.sqrt(HEAD_DIM)
NT_DIMS = (((1,), (1,)), ((), ()))
VMEM_LIMIT = 56 * 1024 * 1024


def _resident(shape):
    return pl.BlockSpec(shape, lambda *_: (0,) * len(shape), pipeline_mode=pl.Buffered(1))


def _log2_sigmoid(z2):
    return jnp.minimum(z2, 0.0) - jnp.log2(1.0 + jnp.exp2(-jnp.abs(z2)))


def _split_bf16(x):
    hi = x.astype(BF16)
    lo = (x - hi.astype(F32)).astype(BF16)
    return hi, lo


def _rms_norm_bf16(x, g):
    ms = jnp.mean(x * x, axis=-1, keepdims=True)
    return (x * lax.rsqrt(ms + RMS_EPS) * g).astype(BF16)


def _proj_body(*refs, kv_transposed, rider):
    pt_ref, refs = (refs[0], refs[1:]) if rider else (None, refs)
    x_ref, g_ref, wq_ref, wkv_ref, wg_ref, cos_ref, sin_ref = refs[:7]
    refs = refs[7:]
    if kv_transposed:
        cos_t_ref, sin_t_ref = refs[:2]
        refs = refs[2:]
    rider_in, refs = (refs[:5], refs[5:]) if rider else (None, refs)
    ka_ref, va_ref, ks_ref, vs_ref, qa_ref, qs_ref, ga_ref, gs_ref = refs[:8]
    refs = refs[8:]
    rider_out, refs = (refs[:1], refs[1:]) if rider else ((), refs)
    h_s, rider_scratch = refs[0], refs[1:]
    d_model = ga_ref.shape[1]

    h_s[...] = _rms_norm_bf16(x_ref[...], g_ref[...])
    cos = cos_ref[...]
    sin = sin_ref[...]
    lane = lax.broadcasted_iota(jnp.int32, cos.shape, 1)
    first_half = (lane % HEAD_DIM) < HALF_DIM

    def rope_slab(xs):
        rot = jnp.where(first_half, pltpu.roll(xs, LANES - HALF_DIM, 1), pltpu.roll(xs, HALF_DIM, 1))
        return xs * cos + rot * sin

    def queries():
        q = jnp.dot(h_s[...], wq_ref[...], preferred_element_type=F32)
        for p in range(N_PAIRS):
            sl = slice(p * LANES, (p + 1) * LANES)
            qa_ref[:, sl] = (rope_slab(q[:, sl]) * SCORE_SCALE).astype(qa_ref.dtype)
        qs_ref[...] = (q[:, W_MIX:] * SCORE_SCALE).astype(qs_ref.dtype)

    def kv_segment(i):
        if kv_transposed:
            return lax.dot_general(wkv_ref[i * W_MIX:(i + 1) * W_MIX, :], h_s[...], NT_DIMS,
                                   preferred_element_type=F32)
        return jnp.dot(h_s[...], wkv_ref[:, i * W_MIX:(i + 1) * W_MIX], preferred_element_type=F32)

    def moba_keys():
        ka = kv_segment(0)
        if kv_transposed:
            cos_t = cos_t_ref[...]
            sin_t = sin_t_ref[...]
            for hd in range(N_HEADS):
                x1 = ka[hd * HEAD_DIM:hd * HEAD_DIM + HALF_DIM, :]
                x2 = ka[hd * HEAD_DIM + HALF_DIM:(hd + 1) * HEAD_DIM, :]
                ka_ref[hd * HEAD_DIM:hd * HEAD_DIM + HALF_DIM, :] = x1 * cos_t - x2 * sin_t
                ka_ref[hd * HEAD_DIM + HALF_DIM:(hd + 1) * HEAD_DIM, :] = x2 * cos_t + x1 * sin_t
        else:
            for p in range(N_PAIRS):
                sl = slice(p * LANES, (p + 1) * LANES)
                ka_ref[:, sl] = rope_slab(ka[:, sl])

    def plain_segment(i, out_ref):
        def run():
            out_ref[...] = kv_segment(i)
        return run

    def gate(lo, out_ref):
        def run():
            g = jnp.dot(h_s[...], wg_ref[:, lo:lo + d_model], preferred_element_type=F32)
            out_ref[...] = jax.nn.sigmoid(g).astype(out_ref.dtype)
        return run

    work = [queries, moba_keys, plain_segment(1, va_ref), plain_segment(2, ks_ref),
            plain_segment(3, vs_ref), gate(0, ga_ref), gate(d_model, gs_ref)]
    _run_with_rider(work, rider, pt_ref, rider_in, rider_out, rider_scratch)


def _proj(x, g_mix, wq, wkv, wg, cos, sin, cos_t=None, sin_t=None, *, bm, seq=None, rider=None):
    m, d = x.shape
    kv_transposed = seq is not None
    row = lambda width: pl.BlockSpec((bm, width), lambda i, *_: (i, 0))
    in_specs = [row(d), _resident((1, d)), _resident(wq.shape), _resident(wkv.shape),
                _resident(wg.shape), row(LANES), row(LANES)]
    args = [x, g_mix, wq, wkv, wg, cos, sin]
    if kv_transposed:
        per_seq = seq // bm
        tab = pl.BlockSpec((HALF_DIM, bm), lambda i, *_: (0, i % per_seq))
        in_specs += [tab, tab]
        args += [cos_t, sin_t]
        kv_shape = jax.ShapeDtypeStruct((m // seq, W_MIX, seq), F32)
        kv_spec = pl.BlockSpec((None, W_MIX, bm), lambda i, *_: (i // per_seq, 0, i % per_seq))
    else:
        kv_shape = jax.ShapeDtypeStruct((m, W_MIX), F32)
        kv_spec = row(W_MIX)
    out_specs = [kv_spec] * 4 + [row(W_MIX)] * 2 + [row(d)] * 2
    out_shape = [kv_shape] * 4 + [jax.ShapeDtypeStruct((m, W_MIX), BF16)] * 2 \
        + [jax.ShapeDtypeStruct((m, d), BF16)] * 2
    return _call_with_rider(
        functools.partial(_proj_body, kv_transposed=kv_transposed), rider,
        grid=(m // bm,), in_specs=in_specs, args=args, out_specs=out_specs, out_shape=out_shape,
        scratch_shapes=[pltpu.VMEM((bm, d), BF16)],
        name="proj_prompt" if kv_transposed else "proj_decode")


FF_CHUNK = 256


def _post_body(*refs, rider):
    pt_ref, refs = (refs[0], refs[1:]) if rider else (None, refs)
    (x_ref, oa_ref, os_ref, ga_ref, gs_ref, wa_ref, ws_ref, wo_ref,
     gf_ref, wg_ref, wu_ref, wd_ref, gfin_ref) = refs[:13]
    refs = refs[13:]
    rider_in, refs = (refs[:5], refs[5:]) if rider else (None, refs)
    y_ref, refs = refs[0], refs[1:]
    rider_out, refs = (refs[:1], refs[1:]) if rider else ((), refs)
    h_s, acc_s, rider_scratch = refs[0], refs[1], refs[2:]
    d_ff = wg_ref.shape[1]

    def merge_branches():
        merged = (ga_ref[...].astype(F32) * jnp.dot(oa_ref[...], wa_ref[...], preferred_element_type=F32)
                  + gs_ref[...].astype(F32) * jnp.dot(os_ref[...], ws_ref[...], preferred_element_type=F32))
        h_s[...] = merged.astype(BF16)

    def out_projection():
        x1 = x_ref[...] + jnp.dot(h_s[...], wo_ref[...], preferred_element_type=F32)
        acc_s[...] = x1
        h_s[...] = _rms_norm_bf16(x1, gf_ref[...])

    def ffn_columns(lo):
        def run():
            h2 = h_s[...]
            gate = jnp.dot(h2, wg_ref[:, lo:lo + FF_CHUNK], preferred_element_type=F32)
            up = jnp.dot(h2, wu_ref[:, lo:lo + FF_CHUNK], preferred_element_type=F32)
            ff = (gate * jax.nn.sigmoid(gate) * up).astype(BF16)
            acc_s[...] += jnp.dot(ff, wd_ref[lo:lo + FF_CHUNK, :], preferred_element_type=F32)
        return run

    def closing_norm():
        x2 = acc_s[...]
        ms2 = jnp.mean(x2 * x2, axis=-1, keepdims=True)
        y_ref[...] = x2 * lax.rsqrt(ms2 + RMS_EPS) * gfin_ref[...]

    work = [merge_branches, out_projection] + [ffn_columns(lo) for lo in range(0, d_ff, FF_CHUNK)] \
        + [closing_norm]
    _run_with_rider(work, rider, pt_ref, rider_in, rider_out, rider_scratch)


def _post(x, oa, os_, ga, gs, wa, ws, wo, g_ffn, wg, wu, wd, g_final, *, bm, rider=None):
    m, d = x.shape
    d_ff = wg.shape[1]
    assert d_ff % FF_CHUNK == 0
    row = lambda width: pl.BlockSpec((bm, width), lambda i, *_: (i, 0))
    return _call_with_rider(
        _post_body, rider, grid=(m // bm,),
        in_specs=[row(d), row(W_MIX), row(W_MIX), row(d), row(d),
                  _resident((W_MIX, d)), _resident((W_MIX, d)), _resident((d, d)),
                  _resident((1, d)), _resident((d, d_ff)), _resident((d, d_ff)),
                  _resident((d_ff, d)), _resident((1, d))],
        args=[x, oa, os_, ga, gs, wa, ws, wo, g_ffn, wg, wu, wd, g_final],
        out_specs=[row(d)], out_shape=[jax.ShapeDtypeStruct((m, d), F32)],
        scratch_shapes=[pltpu.VMEM((bm, d), BF16), pltpu.VMEM((bm, d), F32)],
        name="post_ffn")


GATE_GROUP = 8


def _moba_prompt_body(q_ref, k_ref, v_ref, o_ref,
                      kaug, vb, kmhi, kmlo, lhs, m_s, acc_s, *, seq):
    c = pl.program_id(1)
    nb = seq // MOBA_BLOCK
    assert nb <= GATE_GROUP and N_HEADS * GATE_GROUP <= LANES

    @pl.when(c == 0)
    def _per_sequence_setup():
        r = lax.broadcasted_iota(jnp.int32, (LANES, MOBA_BLOCK), 0)
        lane_w = lax.broadcasted_iota(jnp.int32, (W_MIX, LANES), 1)
        row_w = lax.broadcasted_iota(jnp.int32, (W_MIX, LANES), 0)
        km = jnp.zeros((W_MIX, LANES), F32)
        for t in range(nb):
            sl = slice(t * MOBA_BLOCK, (t + 1) * MOBA_BLOCK)
            ind = jnp.where((r < N_HEADS * GATE_GROUP) & ((r % GATE_GROUP) == t), 1.0, 0.0).astype(BF16)
            for p in range(N_PAIRS):
                kaug[p, t, 0:LANES, :] = k_ref[p * LANES:(p + 1) * LANES, sl].astype(BF16)
                kaug[p, t, LANES:2 * LANES, :] = ind
                vb[p, t, 0:LANES, :] = v_ref[p * LANES:(p + 1) * LANES, sl].astype(BF16)
                vb[p, t, LANES:2 * LANES, :] = jnp.ones((LANES, MOBA_BLOCK), BF16)
            block_sum = jnp.sum(k_ref[:, sl], axis=1, keepdims=True)
            km = jnp.where((lane_w % GATE_GROUP) == t, block_sum, km)
        km = jnp.where((lane_w < N_HEADS * GATE_GROUP) & ((row_w // HEAD_DIM) == (lane_w // GATE_GROUP)),
                       km * (1.0 / MOBA_BLOCK), 0.0)
        hi, lo = _split_bf16(km)
        kmhi[...] = hi
        kmlo[...] = lo

    q = q_ref[...]
    gate = (jnp.dot(q, kmhi[...], preferred_element_type=F32)
            + jnp.dot(q, kmlo[...], preferred_element_type=F32))
    lane = lax.broadcasted_iota(jnp.int32, (Q_BLOCK, LANES), 1)
    n = lane % GATE_GROUP
    cur = (c * Q_BLOCK) // MOBA_BLOCK
    g = jnp.where(n < cur, gate, -jnp.inf)
    rank = jnp.zeros((Q_BLOCK, LANES), F32)
    for r in range(1, GATE_GROUP):
        wraps = (n + r) >= GATE_GROUP
        other = jnp.where(wraps, pltpu.roll(g, GATE_GROUP - r, 1), pltpu.roll(g, LANES - r, 1))
        beats = (other > g) | (wraps & (other == g))
        rank = rank + beats.astype(F32)
    keep = ((n < cur) & (rank < MOBA_TOPK)) | (n >= cur)
    bias = jnp.where(keep | (lane >= N_HEADS * GATE_GROUP), 0.0, NEG_BIG)

    qf = q.astype(F32)
    for h in range(N_HEADS):
        p, j = divmod(h, 2)
        in_head = (lane >= j * HEAD_DIM) & (lane < (j + 1) * HEAD_DIM)
        lhs[h, :, 0:LANES] = jnp.where(in_head, qf[:, p * LANES:(p + 1) * LANES], 0.0).astype(BF16)
        in_group = (lane >= h * GATE_GROUP) & (lane < (h + 1) * GATE_GROUP)
        lhs[h, :, LANES:2 * LANES] = jnp.where(in_group, bias, 0.0).astype(BF16)
    m_s[...] = jnp.full(m_s.shape, NEG_BIG, F32)
    acc_s[...] = jnp.zeros(acc_s.shape, F32)

    qq = lax.broadcasted_iota(jnp.int32, (Q_BLOCK, MOBA_BLOCK), 0)
    kk = lax.broadcasted_iota(jnp.int32, (Q_BLOCK, MOBA_BLOCK), 1)
    causal = kk <= qq + (c * Q_BLOCK - cur * MOBA_BLOCK)
    twice = lambda x: jnp.concatenate([x, x], axis=1)

    def tile_step(t, own):
        heads = range(N_HEADS)
        s = [jnp.dot(lhs[h], kaug[h // 2, t], preferred_element_type=F32) for h in heads]
        if own:
            s = [jnp.where(causal, x, NEG_BIG) for x in s]
        m_old = [m_s[h] for h in heads]
        m_new = [jnp.maximum(m_old[h], jnp.max(s[h], axis=1, keepdims=True)) for h in heads]
        alpha = [jnp.exp2(m_old[h] - m_new[h]) for h in heads]
        pr = [jnp.exp2(s[h] - twice(m_new[h])).astype(BF16) for h in heads]
        for h in heads:
            pv = lax.dot_general(pr[h], vb[h // 2, t], NT_DIMS, preferred_element_type=F32)
            acc_s[h] = twice(alpha[h]) * acc_s[h] + pv
            m_s[h] = m_new[h]

    def past_tile(t, carry):
        tile_step(t, own=False)
        return carry

    lax.fori_loop(0, cur, past_tile, 0)
    tile_step(cur, own=True)

    for p in range(N_PAIRS):
        acc_a, acc_b = acc_s[2 * p], acc_s[2 * p + 1]
        oa = acc_a[:, :LANES] / acc_a[:, LANES:]
        ob = acc_b[:, :LANES] / acc_b[:, LANES:]
        o_ref[:, p * LANES:(p + 1) * LANES] = jnp.where(lane < HEAD_DIM, oa, ob).astype(o_ref.dtype)


def _moba_prompt(qa, ka_t, va_t, *, batch, seq):
    nq = seq // Q_BLOCK
    nb = seq // MOBA_BLOCK
    kv_spec = pl.BlockSpec((None, W_MIX, seq), lambda b, c: (b, 0, 0))
    return pl.pallas_call(
        functools.partial(_moba_prompt_body, seq=seq),
        grid=(batch, nq),
        in_specs=[pl.BlockSpec((Q_BLOCK, W_MIX), lambda b, c: (b * nq + c, 0)), kv_spec, kv_spec],
        out_specs=pl.BlockSpec((Q_BLOCK, W_MIX), lambda b, c: (b * nq + c, 0)),
        out_shape=jax.ShapeDtypeStruct((batch * seq, W_MIX), BF16),
        scratch_shapes=[pltpu.VMEM((N_PAIRS, nb, 2 * LANES, MOBA_BLOCK), BF16),
                        pltpu.VMEM((N_PAIRS, nb, 2 * LANES, MOBA_BLOCK), BF16),
                        pltpu.VMEM((W_MIX, LANES), BF16),
                        pltpu.VMEM((W_MIX, LANES), BF16),
                        pltpu.VMEM((N_HEADS, Q_BLOCK, 2 * LANES), BF16),
                        pltpu.VMEM((N_HEADS, Q_BLOCK, LANES), F32),
                        pltpu.VMEM((N_HEADS, Q_BLOCK, 2 * LANES), F32)],
        compiler_params=pltpu.CompilerParams(
            dimension_semantics=("arbitrary", "arbitrary"), vmem_limit_bytes=VMEM_LIMIT),
        name="moba_prompt",
    )(qa, ka_t, va_t)


SB_TILE = 256


def _suffix_sum_matrix(n):
    r = lax.broadcasted_iota(jnp.int32, (n, n), 0)
    c = lax.broadcasted_iota(jnp.int32, (n, n), 1)
    return jnp.where(r > c, 1.0, 0.0).astype(BF16)


def _sb_log_terms(z2, past, tri):
    ls = _log2_sigmoid(z2)
    lk = ls - z2
    if past is not None:
        lk = jnp.where(past, lk, 0.0)
    tail = jnp.dot(lk.astype(BF16), tri, preferred_element_type=F32)
    return ls, lk, tail


def _sb_weights(z2, past, later, tri):
    ls, lk, tail = _sb_log_terms(z2, past, tri)
    a = jnp.exp2(ls + tail + later)
    if past is not None:
        a = jnp.where(past, a, 0.0)
    return a, jnp.sum(lk, axis=1, keepdims=True)


def _sb_prompt_body(q_ref, k_ref, v_ref, o_ref, kb, vb, tri, later_s, acc_s, *, seq):
    c = pl.program_id(1)

    @pl.when(c == 0)
    def _per_sequence_setup():
        for t in range(seq // SB_TILE):
            sl = slice(t * SB_TILE, (t + 1) * SB_TILE)
            for p in range(N_PAIRS):
                kb[p, t] = k_ref[p * LANES:(p + 1) * LANES, sl].astype(BF16)
                vb[p, t] = v_ref[p * LANES:(p + 1) * LANES, sl].astype(BF16)
        tri[...] = _suffix_sum_matrix(SB_TILE)

    qf = q_ref[...].astype(F32)
    lane = lax.broadcasted_iota(jnp.int32, (Q_BLOCK, LANES), 1)
    qm = []
    for h in range(N_HEADS):
        p, j = divmod(h, 2)
        in_head = (lane >= j * HEAD_DIM) & (lane < (j + 1) * HEAD_DIM)
        qm.append(jnp.where(in_head, qf[:, p * LANES:(p + 1) * LANES], 0.0).astype(BF16))
    later_s[...] = jnp.zeros(later_s.shape, F32)
    acc_s[...] = jnp.zeros(acc_s.shape, F32)

    diag = (c * Q_BLOCK) // SB_TILE
    qq = lax.broadcasted_iota(jnp.int32, (Q_BLOCK, SB_TILE), 0)
    kk = lax.broadcasted_iota(jnp.int32, (Q_BLOCK, SB_TILE), 1)
    past_diag = kk < qq + (c * Q_BLOCK - diag * SB_TILE)

    def tile_step(t, past):
        heads = range(N_HEADS)
        z = [jnp.dot(qm[h], kb[h // 2, t], preferred_element_type=F32) for h in heads]
        ls = [_log2_sigmoid(z[h]) for h in heads]
        lk = [ls[h] - z[h] for h in heads]
        if past is not None:
            lk = [jnp.where(past, x, 0.0) for x in lk]
        u = tri[...]
        tail = [jnp.dot(lk[h].astype(BF16), u, preferred_element_type=F32) for h in heads]
        a = [jnp.exp2(ls[h] + tail[h]) for h in heads]
        if past is not None:
            a = [jnp.where(past, x, 0.0) for x in a]
        for h in heads:
            pv = lax.dot_general(a[h].astype(BF16), vb[h // 2, t], NT_DIMS,
                                 preferred_element_type=F32)
            later = later_s[h]
            acc_s[h] = acc_s[h] + jnp.exp2(later) * pv
            later_s[h] = later + jnp.sum(lk[h], axis=1, keepdims=True)

    tile_step(diag, past_diag)

    def earlier_tile(i, carry):
        tile_step(diag - 1 - i, None)
        return carry

    lax.fori_loop(0, diag, earlier_tile, 0)
    for p in range(N_PAIRS):
        o_ref[:, p * LANES:(p + 1) * LANES] = jnp.where(
            lane < HEAD_DIM, acc_s[2 * p], acc_s[2 * p + 1]).astype(o_ref.dtype)


def _sb_prompt(qs, ks_t, vs_t, *, batch, seq):
    nq = seq // Q_BLOCK
    nt = seq // SB_TILE
    kv_spec = pl.BlockSpec((None, W_MIX, seq), lambda b, c: (b, 0, 0))
    return pl.pallas_call(
        functools.partial(_sb_prompt_body, seq=seq),
        grid=(batch, nq),
        in_specs=[pl.BlockSpec((Q_BLOCK, W_MIX), lambda b, c: (b * nq + c, 0)), kv_spec, kv_spec],
        out_specs=pl.BlockSpec((Q_BLOCK, W_MIX), lambda b, c: (b * nq + c, 0)),
        out_shape=jax.ShapeDtypeStruct((batch * seq, W_MIX), BF16),
        scratch_shapes=[pltpu.VMEM((N_PAIRS, nt, LANES, SB_TILE), BF16),
                        pltpu.VMEM((N_PAIRS, nt, LANES, SB_TILE), BF16),
                        pltpu.VMEM((SB_TILE, SB_TILE), BF16),
                        pltpu.VMEM((N_HEADS, Q_BLOCK, LANES), F32),
                        pltpu.VMEM((N_HEADS, Q_BLOCK, LANES), F32)],
        compiler_params=pltpu.CompilerParams(
            dimension_semantics=("arbitrary", "arbitrary"), vmem_limit_bytes=VMEM_LIMIT),
        name="sb_prompt",
    )(qs, ks_t, vs_t)


PAGES_PER_STEP = 8
STEP_KEYS = PAGES_PER_STEP * PAGE_SIZE
RING_STEPS = 4


class _Rider(NamedTuple):
    mode: str
    page_table: jax.Array
    q: jax.Array
    k_new: jax.Array
    v_new: jax.Array
    cache_k: jax.Array
    cache_v: jax.Array
    n_pages: int


class _RiderConfig(NamedTuple):
    mode: str
    n_pages: int
    dec_seq: int
    n_batch: int


def _decode_scratch(cfg):
    npg = PAGES_PER_STEP
    n_groups = cfg.n_pages // npg
    rows = N_HEADS * cfg.dec_seq
    return [pltpu.VMEM((RING_STEPS * npg, W_MIX, PAGE_SIZE), F32),
            pltpu.SemaphoreType.DMA((RING_STEPS,)),
            pltpu.VMEM((rows, W_MIX), BF16),
            pltpu.VMEM((n_groups, rows, STEP_KEYS), F32),
            pltpu.VMEM((n_groups, rows, STEP_KEYS), BF16),
            pltpu.VMEM((rows, LANES), BF16),
            pltpu.VMEM((rows, 1), F32),
            pltpu.VMEM((rows, W_MIX), F32)]


def _decode_steps(cfg, b, pt_ref, q_ref, kn_ref, vn_ref, ck_hbm, cv_hbm, o_ref, scratch):
    pages, sems, qbd, z_s, a_s, anew_s, norm_s, o_acc = scratch
    mode, n_pages, dec_seq, n_batch = cfg
    npg = PAGES_PER_STEP
    n_groups = n_pages // npg
    n_steps = 2 * n_groups
    lookahead = RING_STEPS - 1
    assert n_steps % RING_STEPS == 0
    rows = N_HEADS * dec_seq
    assert rows <= LANES and dec_seq <= 8

    def page_copy(step, j, batch):
        src = ck_hbm if step < n_groups else cv_hbm
        page = (step % n_groups) * npg + j
        ring = step % RING_STEPS
        return pltpu.make_async_copy(src.at[pt_ref[batch * n_pages + page]],
                                     pages.at[ring * npg + j], sems.at[ring])

    def start_step(step, batch):
        for j in range(npg):
            page_copy(step, j, batch).start()

    def wait_step(step):
        for j in range(npg):
            page_copy(step, j, b).wait()

    def step_pages(step):
        ring = step % RING_STEPS
        return jnp.concatenate([pages[ring * npg + j].astype(BF16) for j in range(npg)], axis=1)

    row_i = lax.broadcasted_iota(jnp.int32, (rows, LANES), 0) % dec_seq
    lane = lax.broadcasted_iota(jnp.int32, (rows, LANES), 1)

    def prime():
        @pl.when(b == 0)
        def _prime_ring():
            for s in range(lookahead):
                start_step(s, 0)

    def block_diagonal_queries():
        qf = q_ref[...].astype(F32)
        qt = jnp.concatenate([qf] * N_HEADS, axis=0)
        r = lax.broadcasted_iota(jnp.int32, qt.shape, 0)
        l = lax.broadcasted_iota(jnp.int32, qt.shape, 1)
        qbd[...] = jnp.where((l // HEAD_DIM) == (r // dec_seq), qt, 0.0).astype(BF16)

    def weights():
        kn = jnp.concatenate([kn_ref[...], jnp.zeros((LANES - dec_seq, W_MIX), F32)], axis=0)
        zn = lax.dot_general(qbd[...], kn.astype(BF16), NT_DIMS, preferred_element_type=F32)
        if mode == "sb":
            tri = _suffix_sum_matrix(SB_TILE)
            past_new = lane < row_i
            a_new, later = _sb_weights(zn, past_new, 0.0, _suffix_sum_matrix(LANES))
            anew_s[...] = a_new.astype(BF16)
            for gg in range(n_groups - 1, -1, -1):
                for t in range(STEP_KEYS // SB_TILE - 1, -1, -1):
                    sl = slice(t * SB_TILE, (t + 1) * SB_TILE)
                    a, tot = _sb_weights(z_s[gg, :, sl], None, later, tri)
                    a_s[gg, :, sl] = a.astype(BF16)
                    later = later + tot
            norm_s[...] = jnp.ones(norm_s.shape, F32)
        else:
            blocks_per_step = STEP_KEYS // MOBA_BLOCK
            n_blocks = n_groups * blocks_per_step
            assert n_blocks <= LANES
            gate = jnp.zeros((rows, LANES), F32)
            for gg in range(n_groups):
                for t in range(blocks_per_step):
                    blk = z_s[gg, :, t * MOBA_BLOCK:(t + 1) * MOBA_BLOCK]
                    gate = jnp.where(lane == gg * blocks_per_step + t,
                                     jnp.sum(blk, axis=1, keepdims=True), gate)
            gm = jnp.where(lane < n_blocks, gate, -jnp.inf)
            sel = jnp.zeros((rows, LANES), F32)
            for _ in range(min(MOBA_TOPK, n_blocks)):
                mx = jnp.max(gm, axis=1, keepdims=True)
                idx = jnp.min(jnp.where(gm == mx, lane, LANES), axis=1, keepdims=True)
                pick = lane == idx
                sel = jnp.where(pick, 1.0, sel)
                gm = jnp.where(pick, -jnp.inf, gm)
            sel_b = sel.astype(BF16)
            own = (lane <= row_i) & (lane < dec_seq)
            zn = jnp.where(own, zn, NEG_BIG)
            mx = jnp.max(zn, axis=1, keepdims=True)
            masked = []
            for gg in range(n_groups):
                bl = lax.broadcasted_iota(jnp.int32, (LANES, STEP_KEYS), 0)
                bk = lax.broadcasted_iota(jnp.int32, (LANES, STEP_KEYS), 1)
                expand = jnp.where(bl == gg * blocks_per_step + bk // MOBA_BLOCK, 1.0, 0.0).astype(BF16)
                chosen = jnp.dot(sel_b, expand, preferred_element_type=F32) > 0.5
                s = jnp.where(chosen, z_s[gg], NEG_BIG)
                masked.append(s)
                mx = jnp.maximum(mx, jnp.max(s, axis=1, keepdims=True))
            p_new = jnp.exp2(zn - mx)
            denom = jnp.sum(p_new, axis=1, keepdims=True)
            anew_s[...] = p_new.astype(BF16)
            for gg in range(n_groups):
                pr = jnp.exp2(masked[gg] - mx)
                denom = denom + jnp.sum(pr, axis=1, keepdims=True)
                a_s[gg] = pr.astype(BF16)
            norm_s[...] = denom

    def write_out():
        o = o_acc[...] / norm_s[...]
        hd = lax.broadcasted_iota(jnp.int32, (dec_seq, W_MIX), 1) // HEAD_DIM
        out = jnp.zeros((dec_seq, W_MIX), F32)
        for h in range(N_HEADS):
            out = jnp.where(hd == h, o[h * dec_seq:(h + 1) * dec_seq, :], out)
        o_ref[...] = out.astype(o_ref.dtype)

    def make_step(step):
        def run():
            ahead = step + lookahead
            if ahead < n_steps:
                start_step(ahead, b)
            else:
                @pl.when(b + 1 < n_batch)
                def _start_next_sequence():
                    start_step(ahead - n_steps, b + 1)
            wait_step(step)
            if step == 0:
                block_diagonal_queries()
            if step < n_groups:
                z_s[step] = jnp.dot(qbd[...], step_pages(step), preferred_element_type=F32)
                if step == n_groups - 1:
                    weights()
                    vn = jnp.concatenate([vn_ref[...], jnp.zeros((LANES - dec_seq, W_MIX), F32)], axis=0)
                    o_acc[...] = jnp.dot(anew_s[...], vn.astype(BF16), preferred_element_type=F32)
            else:
                o_acc[...] = o_acc[...] + lax.dot_general(
                    a_s[step - n_groups], step_pages(step), NT_DIMS, preferred_element_type=F32)
            if step == n_steps - 1:
                write_out()
        return run

    return prime, [make_step(step) for step in range(n_steps)]


def _run_with_rider(work, cfg, pt_ref, rider_in, rider_out, rider_scratch):
    if cfg is None:
        for item in work:
            item()
        return
    prime, steps = _decode_steps(cfg, pl.program_id(0), pt_ref, *rider_in, *rider_out, rider_scratch)
    prime()
    done = 0
    for i, item in enumerate(work):
        item()
        upto = len(steps) * (i + 1) // len(work)
        for step in steps[done:upto]:
            step()
        done = upto


def _call_with_rider(body, rider, *, grid, in_specs, args, out_specs, out_shape, scratch_shapes, name):
    params = pltpu.CompilerParams(dimension_semantics=("arbitrary",) * len(grid),
                                  vmem_limit_bytes=VMEM_LIMIT)
    if rider is None:
        return pl.pallas_call(
            functools.partial(body, rider=None), grid=grid, in_specs=in_specs, out_specs=out_specs,
            out_shape=out_shape, scratch_shapes=scratch_shapes, compiler_params=params, name=name,
        )(*args)
    dec_batch, dec_seq, _ = rider.q.shape
    assert grid == (dec_batch,), "a rider needs one host grid step per decode sequence"
    cfg = _RiderConfig(rider.mode, rider.n_pages, dec_seq, dec_batch)
    small = pl.BlockSpec((None, dec_seq, W_MIX), lambda i, *_: (i, 0, 0))
    hbm = pl.BlockSpec(memory_space=pl.ANY)
    grid_spec = pltpu.PrefetchScalarGridSpec(
        num_scalar_prefetch=1, grid=grid,
        in_specs=list(in_specs) + [small, small, small, hbm, hbm],
        out_specs=list(out_specs) + [small],
        scratch_shapes=list(scratch_shapes) + _decode_scratch(cfg))
    return pl.pallas_call(
        functools.partial(body, rider=cfg), grid_spec=grid_spec,
        out_shape=list(out_shape) + [jax.ShapeDtypeStruct((dec_batch, dec_seq, W_MIX), BF16)],
        compiler_params=params, name=name + "_decode_" + rider.mode,
    )(rider.page_table, *args, rider.q, rider.k_new, rider.v_new, rider.cache_k, rider.cache_v)


def _rope_tables(pos):
    inv = ROPE_THETA ** (-jnp.arange(HALF_DIM, dtype=F32) / HALF_DIM)
    ang = pos.astype(F32)[:, None] * inv[None, :]
    cos, sin = jnp.cos(ang), jnp.sin(ang)
    cos_lane = jnp.tile(cos, (1, LANES // HALF_DIM))
    sin_lane = jnp.tile(jnp.concatenate([-sin, sin], axis=1), (1, LANES // HEAD_DIM))
    return cos_lane, sin_lane, cos.T, sin.T


def kernel(x_prompt, x_sample, cache_moba_k, cache_moba_v, cache_sb_k, cache_sb_v, page_table,
           g_mix, w_in, w_branch_moba, w_branch_sb, w_out, g_ffn, w_ffn_gate, w_ffn_up,
           w_ffn_down, g_final):
    batch, seq, d = x_prompt.shape
    dec_batch, dec_seq, _ = x_sample.shape
    depth = w_in.shape[0]
    n_pages = page_table.shape[1]
    past_len = n_pages * PAGE_SIZE
    n_pool = cache_moba_k.shape[1]
    assert seq % MOBA_BLOCK == 0 and past_len % MOBA_BLOCK == 0 and n_pages % PAGES_PER_STEP == 0
    assert (batch * seq) % dec_batch == 0 and seq % ((batch * seq) // dec_batch) == 0
    assert depth == 1

    cos_p, sin_p, cos_pt, sin_pt = _rope_tables(jnp.arange(seq, dtype=jnp.int32))
    cos_p, sin_p = jnp.tile(cos_p, (batch, 1)), jnp.tile(sin_p, (batch, 1))
    cos_s, sin_s, _, _ = _rope_tables(past_len + jnp.arange(dec_seq, dtype=jnp.int32))
    cos_s, sin_s = jnp.tile(cos_s, (dec_batch, 1)), jnp.tile(sin_s, (dec_batch, 1))
    pt_flat = page_table.reshape(-1)

    xp = x_prompt.reshape(batch * seq, d)
    xs = x_sample.reshape(dec_batch * dec_seq, d)
    m_s = dec_batch * dec_seq
    g_last = g_final.reshape(1, d)
    l = 0
    w_in_b = w_in[l].astype(BF16)
    wq = jnp.concatenate([w_in_b[:, 0:W_MIX], w_in_b[:, 3 * W_MIX:4 * W_MIX]], axis=1)
    wkv = jnp.concatenate([w_in_b[:, W_MIX:3 * W_MIX], w_in_b[:, 4 * W_MIX:6 * W_MIX]], axis=1)
    wgt = w_in_b[:, 6 * W_MIX:]
    bf = lambda w: w[l].astype(BF16)
    wa, ws, wo = bf(w_branch_moba), bf(w_branch_sb), bf(w_out)
    wg, wu, wd = bf(w_ffn_gate), bf(w_ffn_up), bf(w_ffn_down)
    gm, gf = g_mix[l].reshape(1, d), g_ffn[l].reshape(1, d)

    ka2, va2, ks2, vs2, qa2, qs2, ga2, gs2 = _proj(xs, gm, wq, wkv, wgt, cos_s, sin_s, bm=m_s)
    r3 = lambda t: t.reshape(dec_batch, dec_seq, W_MIX)
    pool = lambda cch: cch[l].transpose(0, 2, 3, 1).reshape(n_pool, W_MIX, PAGE_SIZE)
    moba_rider = _Rider("moba", pt_flat, r3(qa2), r3(ka2), r3(va2), pool(cache_moba_k),
                        pool(cache_moba_v), n_pages)
    sb_rider = _Rider("sb", pt_flat, r3(qs2), r3(ks2), r3(vs2), pool(cache_sb_k),
                      pool(cache_sb_v), n_pages)

    bm = (batch * seq) // dec_batch
    ka_t, va_t, ks_t, vs_t, qa, qs, ga, gs, oa2 = _proj(
        xp, gm, wq, wkv.T, wgt, cos_p, sin_p, cos_pt, sin_pt, bm=bm, seq=seq, rider=moba_rider)
    oa = _moba_prompt(qa, ka_t, va_t, batch=batch, seq=seq)
    os_ = _sb_prompt(qs, ks_t, vs_t, batch=batch, seq=seq)
    y_prompt, os2 = _post(xp, oa, os_, ga, gs, wa, ws, wo, gf, wg, wu, wd, g_last, bm=bm,
                          rider=sb_rider)

    (y_sample,) = _post(xs, oa2.reshape(m_s, W_MIX), os2.reshape(m_s, W_MIX), ga2, gs2,
                        wa, ws, wo, gf, wg, wu, wd, g_last, bm=m_s)

    rows_p = lambda t: t.reshape(1, batch, N_HEADS, HEAD_DIM, seq).transpose(0, 1, 4, 2, 3)
    rows_s = lambda t: t.reshape(1, dec_batch, dec_seq, N_HEADS, HEAD_DIM)
    return (y_prompt.reshape(batch, seq, d), y_sample.reshape(dec_batch, dec_seq, d),
            rows_p(ka_t), rows_p(va_t), rows_p(ks_t), rows_p(vs_t),
            rows_s(ka2), rows_s(va2), rows_s(ks2), rows_s(vs2))
```

```python
import functools
import math
from typing import NamedTuple

import jax
import jax.numpy as jnp
from jax import lax
from jax.experimental import pallas as pl
from jax.experimental.pallas import tpu as pltpu

F32 = jnp.float32
BF16 = jnp.bfloat16

HEAD_DIM = 64
HALF_DIM = HEAD_DIM // 2
N_HEADS = 8
W_MIX = N_HEADS * HEAD_DIM
LANES = 128
N_PAIRS = W_MIX // LANES
MOBA_BLOCK = 256
MOBA_TOPK = 3
Q_BLOCK = 128
PAGE_SIZE = 128
ROPE_THETA = 10000.0
RMS_EPS = 1e-6
NEG_BIG = -1e30
SCORE_SCALE = math.log2(math.e) / math.sqrt(HEAD_DIM)
NT_DIMS = (((1,), (1,)), ((), ()))
VMEM_LIMIT = 56 * 1024 * 1024


def _resident(shape):
    return pl.BlockSpec(shape, lambda *_: (0,) * len(shape), pipeline_mode=pl.Buffered(1))


def _log2_sigmoid(z2):
    return jnp.minimum(z2, 0.0) - jnp.log2(1.0 + jnp.exp2(-jnp.abs(z2)))


def _split_bf16(x):
    hi = x.astype(BF16)
    lo = (x - hi.astype(F32)).astype(BF16)
    return hi, lo


def _rms_norm_bf16(x, g):
    ms = jnp.mean(x * x, axis=-1, keepdims=True)
    return (x * lax.rsqrt(ms + RMS_EPS) * g).astype(BF16)


def _proj_body(*refs, kv_transposed, rider):
    pt_ref, refs = (refs[0], refs[1:]) if rider else (None, refs)
    x_ref, g_ref, wq_ref, wkv_ref, wg_ref, cos_ref, sin_ref = refs[:7]
    refs = refs[7:]
    if kv_transposed:
        cos_t_ref, sin_t_ref = refs[:2]
        refs = refs[2:]
    rider_in, refs = (refs[:5], refs[5:]) if rider else (None, refs)
    ka_ref, va_ref, ks_ref, vs_ref, qa_ref, qs_ref, ga_ref, gs_ref = refs[:8]
    refs = refs[8:]
    rider_out, refs = (refs[:1], refs[1:]) if rider else ((), refs)
    h_s, rider_scratch = refs[0], refs[1:]
    d_model = ga_ref.shape[1]

    h_s[...] = _rms_norm_bf16(x_ref[...], g_ref[...])
    cos = cos_ref[...]
    sin = sin_ref[...]
    lane = lax.broadcasted_iota(jnp.int32, cos.shape, 1)
    first_half = (lane % HEAD_DIM) < HALF_DIM

    def rope_slab(xs):
        rot = jnp.where(first_half, pltpu.roll(xs, LANES - HALF_DIM, 1), pltpu.roll(xs, HALF_DIM, 1))
        return xs * cos + rot * sin

    def queries():
        q = jnp.dot(h_s[...], wq_ref[...], preferred_element_type=F32)
        for p in range(N_PAIRS):
            sl = slice(p * LANES, (p + 1) * LANES)
            qa_ref[:, sl] = (rope_slab(q[:, sl]) * SCORE_SCALE).astype(qa_ref.dtype)
        qs_ref[...] = (q[:, W_MIX:] * SCORE_SCALE).astype(qs_ref.dtype)

    def kv_segment(i):
        if kv_transposed:
            return lax.dot_general(wkv_ref[i * W_MIX:(i + 1) * W_MIX, :], h_s[...], NT_DIMS,
                                   preferred_element_type=F32)
        return jnp.dot(h_s[...], wkv_ref[:, i * W_MIX:(i + 1) * W_MIX], preferred_element_type=F32)

    def moba_keys():
        ka = kv_segment(0)
        if kv_transposed:
            cos_t = cos_t_ref[...]
            sin_t = sin_t_ref[...]
            for hd in range(N_HEADS):
                x1 = ka[hd * HEAD_DIM:hd * HEAD_DIM + HALF_DIM, :]
                x2 = ka[hd * HEAD_DIM + HALF_DIM:(hd + 1) * HEAD_DIM, :]
                ka_ref[hd * HEAD_DIM:hd * HEAD_DIM + HALF_DIM, :] = x1 * cos_t - x2 * sin_t
                ka_ref[hd * HEAD_DIM + HALF_DIM:(hd + 1) * HEAD_DIM, :] = x2 * cos_t + x1 * sin_t
        else:
            for p in range(N_PAIRS):
                sl = slice(p * LANES, (p + 1) * LANES)
                ka_ref[:, sl] = rope_slab(ka[:, sl])

    def plain_segment(i, out_ref):
        def run():
            out_ref[...] = kv_segment(i)
        return run

    def gate(lo, out_ref):
        def run():
            g = jnp.dot(h_s[...], wg_ref[:, lo:lo + d_model], preferred_element_type=F32)
            out_ref[...] = jax.nn.sigmoid(g).astype(out_ref.dtype)
        return run

    work = [queries, moba_keys, plain_segment(1, va_ref), plain_segment(2, ks_ref),
            plain_segment(3, vs_ref), gate(0, ga_ref), gate(d_model, gs_ref)]
    _run_with_rider(work, rider, pt_ref, rider_in, rider_out, rider_scratch)


def _proj(x, g_mix, wq, wkv, wg, cos, sin, cos_t=None, sin_t=None, *, bm, seq=None, rider=None):
    m, d = x.shape
    kv_transposed = seq is not None
    row = lambda width: pl.BlockSpec((bm, width), lambda i, *_: (i, 0))
    in_specs = [row(d), _resident((1, d)), _resident(wq.shape), _resident(wkv.shape),
                _resident(wg.shape), row(LANES), row(LANES)]
    args = [x, g_mix, wq, wkv, wg, cos, sin]
    if kv_transposed:
        per_seq = seq // bm
        tab = pl.BlockSpec((HALF_DIM, bm), lambda i, *_: (0, i % per_seq))
        in_specs += [tab, tab]
        args += [cos_t, sin_t]
        kv_shape = jax.ShapeDtypeStruct((m // seq, W_MIX, seq), F32)
        kv_spec = pl.BlockSpec((None, W_MIX, bm), lambda i, *_: (i // per_seq, 0, i % per_seq))
    else:
        kv_shape = jax.ShapeDtypeStruct((m, W_MIX), F32)
        kv_spec = row(W_MIX)
    out_specs = [kv_spec] * 4 + [row(W_MIX)] * 2 + [row(d)] * 2
    out_shape = [kv_shape] * 4 + [jax.ShapeDtypeStruct((m, W_MIX), BF16)] * 2 \
        + [jax.ShapeDtypeStruct((m, d), BF16)] * 2
    return _call_with_rider(
        functools.partial(_proj_body, kv_transposed=kv_transposed), rider,
        grid=(m // bm,), in_specs=in_specs, args=args, out_specs=out_specs, out_shape=out_shape,
        scratch_shapes=[pltpu.VMEM((bm, d), BF16)],
        name="proj_prompt" if kv_transposed else "proj_decode")


FF_CHUNK = 256


def _post_body(*refs, rider):
    pt_ref, refs = (refs[0], refs[1:]) if rider else (None, refs)
    (x_ref, oa_ref, os_ref, ga_ref, gs_ref, wa_ref, ws_ref, wo_ref,
     gf_ref, wg_ref, wu_ref, wd_ref, gfin_ref) = refs[:13]
    refs = refs[13:]
    rider_in, refs = (refs[:5], refs[5:]) if rider else (None, refs)
    y_ref, refs = refs[0], refs[1:]
    rider_out, refs = (refs[:1], refs[1:]) if rider else ((), refs)
    h_s, acc_s, rider_scratch = refs[0], refs[1], refs[2:]
    d_ff = wg_ref.shape[1]

    def merge_branches():
        merged = (ga_ref[...].astype(F32) * jnp.dot(oa_ref[...], wa_ref[...], preferred_element_type=F32)
                  + gs_ref[...].astype(F32) * jnp.dot(os_ref[...], ws_ref[...], preferred_element_type=F32))
        h_s[...] = merged.astype(BF16)

    def out_projection():
        x1 = x_ref[...] + jnp.dot(h_s[...], wo_ref[...], preferred_element_type=F32)
        acc_s[...] = x1
        h_s[...] = _rms_norm_bf16(x1, gf_ref[...])

    def ffn_columns(lo):
        def run():
            h2 = h_s[...]
            gate = jnp.dot(h2, wg_ref[:, lo:lo + FF_CHUNK], preferred_element_type=F32)
            up = jnp.dot(h2, wu_ref[:, lo:lo + FF_CHUNK], preferred_element_type=F32)
            ff = (gate * jax.nn.sigmoid(gate) * up).astype(BF16)
            acc_s[...] += jnp.dot(ff, wd_ref[lo:lo + FF_CHUNK, :], preferred_element_type=F32)
        return run

    def closing_norm():
        x2 = acc_s[...]
        ms2 = jnp.mean(x2 * x2, axis=-1, keepdims=True)
        y_ref[...] = x2 * lax.rsqrt(ms2 + RMS_EPS) * gfin_ref[...]

    work = [merge_branches, out_projection] + [ffn_columns(lo) for lo in range(0, d_ff, FF_CHUNK)] \
        + [closing_norm]
    _run_with_rider(work, rider, pt_ref, rider_in, rider_out, rider_scratch)


def _post(x, oa, os_, ga, gs, wa, ws, wo, g_ffn, wg, wu, wd, g_final, *, bm, rider=None):
    m, d = x.shape
    d_ff = wg.shape[1]
    assert d_ff % FF_CHUNK == 0
    row = lambda width: pl.BlockSpec((bm, width), lambda i, *_: (i, 0))
    return _call_with_rider(
        _post_body, rider, grid=(m // bm,),
        in_specs=[row(d), row(W_MIX), row(W_MIX), row(d), row(d),
                  _resident((W_MIX, d)), _resident((W_MIX, d)), _resident((d, d)),
                  _resident((1, d)), _resident((d, d_ff)), _resident((d, d_ff)),
                  _resident((d_ff, d)), _resident((1, d))],
        args=[x, oa, os_, ga, gs, wa, ws, wo, g_ffn, wg, wu, wd, g_final],
        out_specs=[row(d)], out_shape=[jax.ShapeDtypeStruct((m, d), F32)],
        scratch_shapes=[pltpu.VMEM((bm, d), BF16), pltpu.VMEM((bm, d), F32)],
        name="post_ffn")


GATE_GROUP = 8


def _moba_prompt_body(q_ref, k_ref, v_ref, o_ref,
                      kaug, vb, kmhi, kmlo, lhs, m_s, acc_s, *, seq):
    c = pl.program_id(1)
    nb = seq // MOBA_BLOCK
    assert nb <= GATE_GROUP and N_HEADS * GATE_GROUP <= LANES

    @pl.when(c == 0)
    def _per_sequence_setup():
        r = lax.broadcasted_iota(jnp.int32, (LANES, MOBA_BLOCK), 0)
        lane_w = lax.broadcasted_iota(jnp.int32, (W_MIX, LANES), 1)
        row_w = lax.broadcasted_iota(jnp.int32, (W_MIX, LANES), 0)
        km = jnp.zeros((W_MIX, LANES), F32)
        for t in range(nb):
            sl = slice(t * MOBA_BLOCK, (t + 1) * MOBA_BLOCK)
            ind = jnp.where((r < N_HEADS * GATE_GROUP) & ((r % GATE_GROUP) == t), 1.0, 0.0).astype(BF16)
            for p in range(N_PAIRS):
                kaug[p, t, 0:LANES, :] = k_ref[p * LANES:(p + 1) * LANES, sl].astype(BF16)
                kaug[p, t, LANES:2 * LANES, :] = ind
                vb[p, t, 0:LANES, :] = v_ref[p * LANES:(p + 1) * LANES, sl].astype(BF16)
                vb[p, t, LANES:2 * LANES, :] = jnp.ones((LANES, MOBA_BLOCK), BF16)
            block_sum = jnp.sum(k_ref[:, sl], axis=1, keepdims=True)
            km = jnp.where((lane_w % GATE_GROUP) == t, block_sum, km)
        km = jnp.where((lane_w < N_HEADS * GATE_GROUP) & ((row_w // HEAD_DIM) == (lane_w // GATE_GROUP)),
                       km * (1.0 / MOBA_BLOCK), 0.0)
        hi, lo = _split_bf16(km)
        kmhi[...] = hi
        kmlo[...] = lo

    q = q_ref[...]
    gate = (jnp.dot(q, kmhi[...], preferred_element_type=F32)
            + jnp.dot(q, kmlo[...], preferred_element_type=F32))
    lane = lax.broadcasted_iota(jnp.int32, (Q_BLOCK, LANES), 1)
    n = lane % GATE_GROUP
    cur = (c * Q_BLOCK) // MOBA_BLOCK
    g = jnp.where(n < cur, gate, -jnp.inf)
    rank = jnp.zeros((Q_BLOCK, LANES), F32)
    for r in range(1, GATE_GROUP):
        wraps = (n + r) >= GATE_GROUP
        other = jnp.where(wraps, pltpu.roll(g, GATE_GROUP - r, 1), pltpu.roll(g, LANES - r, 1))
        beats = (other > g) | (wraps & (other == g))
        rank = rank + beats.astype(F32)
    keep = ((n < cur) & (rank < MOBA_TOPK)) | (n >= cur)
    bias = jnp.where(keep | (lane >= N_HEADS * GATE_GROUP), 0.0, NEG_BIG)

    qf = q.astype(F32)
    for h in range(N_HEADS):
        p, j = divmod(h, 2)
        in_head = (lane >= j * HEAD_DIM) & (lane < (j + 1) * HEAD_DIM)
        lhs[h, :, 0:LANES] = jnp.where(in_head, qf[:, p * LANES:(p + 1) * LANES], 0.0).astype(BF16)
        in_group = (lane >= h * GATE_GROUP) & (lane < (h + 1) * GATE_GROUP)
        lhs[h, :, LANES:2 * LANES] = jnp.where(in_group, bias, 0.0).astype(BF16)
    m_s[...] = jnp.full(m_s.shape, NEG_BIG, F32)
    acc_s[...] = jnp.zeros(acc_s.shape, F32)

    qq = lax.broadcasted_iota(jnp.int32, (Q_BLOCK, MOBA_BLOCK), 0)
    kk = lax.broadcasted_iota(jnp.int32, (Q_BLOCK, MOBA_BLOCK), 1)
    causal = kk <= qq + (c * Q_BLOCK - cur * MOBA_BLOCK)
    twice = lambda x: jnp.concatenate([x, x], axis=1)

    def tile_group(t0, n_tiles, own_last):
        heads = range(N_HEADS)
        tiles = range(n_tiles)
        s = [[jnp.dot(lhs[h], kaug[h // 2, t0 + i], preferred_element_type=F32) for i in tiles]
             for h in heads]
        if own_last:
            for h in heads:
                s[h][-1] = jnp.where(causal, s[h][-1], NEG_BIG)
        m_old = [m_s[h] for h in heads]
        m_new = []
        for h in heads:
            top = functools.reduce(jnp.maximum, s[h])
            m_new.append(jnp.maximum(m_old[h], jnp.max(top, axis=1, keepdims=True)))
        alpha = [jnp.exp2(m_old[h] - m_new[h]) for h in heads]
        pr = [[jnp.exp2(s[h][i] - twice(m_new[h])).astype(BF16) for i in tiles] for h in heads]
        for h in heads:
            pv = sum(lax.dot_general(pr[h][i], vb[h // 2, t0 + i], NT_DIMS,
                                     preferred_element_type=F32) for i in tiles)
            acc_s[h] = twice(alpha[h]) * acc_s[h] + pv
            m_s[h] = m_new[h]

    def past_pair(i, carry):
        tile_group(2 * i, 2, own_last=False)
        return carry

    lax.fori_loop(0, cur // 2, past_pair, 0)

    @pl.when(cur % 2 == 1)
    def _last_past_tile_with_own():
        tile_group(cur - 1, 2, own_last=True)

    @pl.when(cur % 2 == 0)
    def _own_tile_alone():
        tile_group(cur, 1, own_last=True)

    for p in range(N_PAIRS):
        acc_a, acc_b = acc_s[2 * p], acc_s[2 * p + 1]
        oa = acc_a[:, :LANES] / acc_a[:, LANES:]
        ob = acc_b[:, :LANES] / acc_b[:, LANES:]
        o_ref[:, p * LANES:(p + 1) * LANES] = jnp.where(lane < HEAD_DIM, oa, ob).astype(o_ref.dtype)


def _moba_prompt(qa, ka_t, va_t, *, batch, seq):
    nq = seq // Q_BLOCK
    nb = seq // MOBA_BLOCK
    kv_spec = pl.BlockSpec((None, W_MIX, seq), lambda b, c: (b, 0, 0))
    return pl.pallas_call(
        functools.partial(_moba_prompt_body, seq=seq),
        grid=(batch, nq),
        in_specs=[pl.BlockSpec((Q_BLOCK, W_MIX), lambda b, c: (b * nq + c, 0)), kv_spec, kv_spec],
        out_specs=pl.BlockSpec((Q_BLOCK, W_MIX), lambda b, c: (b * nq + c, 0)),
        out_shape=jax.ShapeDtypeStruct((batch * seq, W_MIX), BF16),
        scratch_shapes=[pltpu.VMEM((N_PAIRS, nb, 2 * LANES, MOBA_BLOCK), BF16),
                        pltpu.VMEM((N_PAIRS, nb, 2 * LANES, MOBA_BLOCK), BF16),
                        pltpu.VMEM((W_MIX, LANES), BF16),
                        pltpu.VMEM((W_MIX, LANES), BF16),
                        pltpu.VMEM((N_HEADS, Q_BLOCK, 2 * LANES), BF16),
                        pltpu.VMEM((N_HEADS, Q_BLOCK, LANES), F32),
                        pltpu.VMEM((N_HEADS, Q_BLOCK, 2 * LANES), F32)],
        compiler_params=pltpu.CompilerParams(
            dimension_semantics=("arbitrary", "arbitrary"), vmem_limit_bytes=VMEM_LIMIT),
        name="moba_prompt",
    )(qa, ka_t, va_t)


SB_TILE = 256


def _suffix_sum_matrix(n):
    r = lax.broadcasted_iota(jnp.int32, (n, n), 0)
    c = lax.broadcasted_iota(jnp.int32, (n, n), 1)
    return jnp.where(r > c, 1.0, 0.0).astype(BF16)


def _sb_log_terms(z2, past, tri):
    ls = _log2_sigmoid(z2)
    lk = ls - z2
    if past is not None:
        lk = jnp.where(past, lk, 0.0)
    tail = jnp.dot(lk.astype(BF16), tri, preferred_element_type=F32)
    return ls, lk, tail


def _sb_weights(z2, past, later, tri):
    ls, lk, tail = _sb_log_terms(z2, past, tri)
    a = jnp.exp2(ls + tail + later)
    if past is not None:
        a = jnp.where(past, a, 0.0)
    return a, jnp.sum(lk, axis=1, keepdims=True)


def _sb_prompt_body(q_ref, k_ref, v_ref, o_ref, kb, vb, tri, later_s, acc_s, *, seq):
    c = pl.program_id(1)

    @pl.when(c == 0)
    def _per_sequence_setup():
        for t in range(seq // SB_TILE):
            sl = slice(t * SB_TILE, (t + 1) * SB_TILE)
            for p in range(N_PAIRS):
                kb[p, t] = k_ref[p * LANES:(p + 1) * LANES, sl].astype(BF16)
                vb[p, t] = v_ref[p * LANES:(p + 1) * LANES, sl].astype(BF16)
        tri[...] = _suffix_sum_matrix(SB_TILE)

    qf = q_ref[...].astype(F32)
    lane = lax.broadcasted_iota(jnp.int32, (Q_BLOCK, LANES), 1)
    qm = []
    for h in range(N_HEADS):
        p, j = divmod(h, 2)
        in_head = (lane >= j * HEAD_DIM) & (lane < (j + 1) * HEAD_DIM)
        qm.append(jnp.where(in_head, qf[:, p * LANES:(p + 1) * LANES], 0.0).astype(BF16))
    later_s[...] = jnp.zeros(later_s.shape, F32)
    acc_s[...] = jnp.zeros(acc_s.shape, F32)

    diag = (c * Q_BLOCK) // SB_TILE
    qq = lax.broadcasted_iota(jnp.int32, (Q_BLOCK, SB_TILE), 0)
    kk = lax.broadcasted_iota(jnp.int32, (Q_BLOCK, SB_TILE), 1)
    past_diag = kk < qq + (c * Q_BLOCK - diag * SB_TILE)

    def tile_group(t0, n_tiles, diag_last):
        heads = range(N_HEADS)
        tiles = range(n_tiles)
        masked = lambda i: diag_last and i == n_tiles - 1
        z = [[jnp.dot(qm[h], kb[h // 2, t0 + i], preferred_element_type=F32) for i in tiles]
             for h in heads]
        ls = [[_log2_sigmoid(z[h][i]) for i in tiles] for h in heads]
        lk = [[ls[h][i] - z[h][i] for i in tiles] for h in heads]
        lk = [[jnp.where(past_diag, lk[h][i], 0.0) if masked(i) else lk[h][i] for i in tiles]
              for h in heads]
        u = tri[...]
        tail = [[jnp.dot(lk[h][i].astype(BF16), u, preferred_element_type=F32) for i in tiles]
                for h in heads]
        a = [[jnp.exp2(ls[h][i] + tail[h][i]) for i in tiles] for h in heads]
        a = [[jnp.where(past_diag, a[h][i], 0.0) if masked(i) else a[h][i] for i in tiles]
             for h in heads]
        for h in heads:
            later = later_s[h]
            out = acc_s[h]
            for i in reversed(tiles):
                pv = lax.dot_general(a[h][i].astype(BF16), vb[h // 2, t0 + i], NT_DIMS,
                                     preferred_element_type=F32)
                out = out + jnp.exp2(later) * pv
                later = later + jnp.sum(lk[h][i], axis=1, keepdims=True)
            acc_s[h] = out
            later_s[h] = later

    @pl.when(diag % 2 == 0)
    def _diagonal_alone():
        tile_group(diag, 1, diag_last=True)

    @pl.when(diag % 2 == 1)
    def _diagonal_with_previous():
        tile_group(diag - 1, 2, diag_last=True)

    n_pairs = diag // 2

    def earlier_pair(i, carry):
        tile_group(2 * (n_pairs - 1 - i), 2, diag_last=False)
        return carry

    lax.fori_loop(0, n_pairs, earlier_pair, 0)
    for p in range(N_PAIRS):
        o_ref[:, p * LANES:(p + 1) * LANES] = jnp.where(
            lane < HEAD_DIM, acc_s[2 * p], acc_s[2 * p + 1]).astype(o_ref.dtype)


def _sb_prompt(qs, ks_t, vs_t, *, batch, seq):
    nq = seq // Q_BLOCK
    nt = seq // SB_TILE
    kv_spec = pl.BlockSpec((None, W_MIX, seq), lambda b, c: (b, 0, 0))
    return pl.pallas_call(
        functools.partial(_sb_prompt_body, seq=seq),
        grid=(batch, nq),
        in_specs=[pl.BlockSpec((Q_BLOCK, W_MIX), lambda b, c: (b * nq + c, 0)), kv_spec, kv_spec],
        out_specs=pl.BlockSpec((Q_BLOCK, W_MIX), lambda b, c: (b * nq + c, 0)),
        out_shape=jax.ShapeDtypeStruct((batch * seq, W_MIX), BF16),
        scratch_shapes=[pltpu.VMEM((N_PAIRS, nt, LANES, SB_TILE), BF16),
                        pltpu.VMEM((N_PAIRS, nt, LANES, SB_TILE), BF16),
                        pltpu.VMEM((SB_TILE, SB_TILE), BF16),
                        pltpu.VMEM((N_HEADS, Q_BLOCK, LANES), F32),
                        pltpu.VMEM((N_HEADS, Q_BLOCK, LANES), F32)],
        compiler_params=pltpu.CompilerParams(
            dimension_semantics=("arbitrary", "arbitrary"), vmem_limit_bytes=VMEM_LIMIT),
        name="sb_prompt",
    )(qs, ks_t, vs_t)


PAGES_PER_STEP = 8
STEP_KEYS = PAGES_PER_STEP * PAGE_SIZE
PROJ_RING_STEPS = 8
POST_RING_STEPS = 4


class _Rider(NamedTuple):
    mode: str
    page_table: jax.Array
    q: jax.Array
    k_new: jax.Array
    v_new: jax.Array
    cache_k: jax.Array
    cache_v: jax.Array
    n_pages: int
    ring_steps: int


class _RiderConfig(NamedTuple):
    mode: str
    n_pages: int
    dec_seq: int
    n_batch: int
    ring_steps: int


def _decode_scratch(cfg):
    npg = PAGES_PER_STEP
    n_groups = cfg.n_pages // npg
    rows = N_HEADS * cfg.dec_seq
    return [pltpu.VMEM((cfg.ring_steps * npg, W_MIX, PAGE_SIZE), F32),
            pltpu.SemaphoreType.DMA((cfg.ring_steps,)),
            pltpu.VMEM((rows, W_MIX), BF16),
            pltpu.VMEM((n_groups, rows, STEP_KEYS), F32),
            pltpu.VMEM((n_groups, rows, STEP_KEYS), BF16),
            pltpu.VMEM((rows, LANES), BF16),
            pltpu.VMEM((rows, 1), F32),
            pltpu.VMEM((rows, W_MIX), F32)]


def _decode_steps(cfg, b, pt_ref, q_ref, kn_ref, vn_ref, ck_hbm, cv_hbm, o_ref, scratch):
    pages, sems, qbd, z_s, a_s, anew_s, norm_s, o_acc = scratch
    mode, n_pages, dec_seq, n_batch, ring_steps = cfg
    npg = PAGES_PER_STEP
    n_groups = n_pages // npg
    n_steps = 2 * n_groups
    lookahead = ring_steps - 1
    assert n_steps % ring_steps == 0
    rows = N_HEADS * dec_seq
    assert rows <= LANES and dec_seq <= 8

    def page_copy(step, j, batch):
        src = ck_hbm if step < n_groups else cv_hbm
        page = (step % n_groups) * npg + j
        ring = step % ring_steps
        return pltpu.make_async_copy(src.at[pt_ref[batch * n_pages + page]],
                                     pages.at[ring * npg + j], sems.at[ring])

    def start_step(step, batch):
        for j in range(npg):
            page_copy(step, j, batch).start()

    def wait_step(step):
        for j in range(npg):
            page_copy(step, j, b).wait()

    def step_pages(step):
        ring = step % ring_steps
        return jnp.concatenate([pages[ring * npg + j].astype(BF16) for j in range(npg)], axis=1)

    row_i = lax.broadcasted_iota(jnp.int32, (rows, LANES), 0) % dec_seq
    lane = lax.broadcasted_iota(jnp.int32, (rows, LANES), 1)

    def prime():
        @pl.when(b == 0)
        def _prime_ring():
            for s in range(lookahead):
                start_step(s, 0)

    def block_diagonal_queries():
        qf = q_ref[...].astype(F32)
        qt = jnp.concatenate([qf] * N_HEADS, axis=0)
        r = lax.broadcasted_iota(jnp.int32, qt.shape, 0)
        l = lax.broadcasted_iota(jnp.int32, qt.shape, 1)
        qbd[...] = jnp.where((l // HEAD_DIM) == (r // dec_seq), qt, 0.0).astype(BF16)

    def weights():
        kn = jnp.concatenate([kn_ref[...], jnp.zeros((LANES - dec_seq, W_MIX), F32)], axis=0)
        zn = lax.dot_general(qbd[...], kn.astype(BF16), NT_DIMS, preferred_element_type=F32)
        if mode == "sb":
            tri = _suffix_sum_matrix(SB_TILE)
            past_new = lane < row_i
            a_new, later = _sb_weights(zn, past_new, 0.0, _suffix_sum_matrix(LANES))
            anew_s[...] = a_new.astype(BF16)
            for gg in range(n_groups - 1, -1, -1):
                for t in range(STEP_KEYS // SB_TILE - 1, -1, -1):
                    sl = slice(t * SB_TILE, (t + 1) * SB_TILE)
                    a, tot = _sb_weights(z_s[gg, :, sl], None, later, tri)
                    a_s[gg, :, sl] = a.astype(BF16)
                    later = later + tot
            norm_s[...] = jnp.ones(norm_s.shape, F32)
        else:
            blocks_per_step = STEP_KEYS // MOBA_BLOCK
            n_blocks = n_groups * blocks_per_step
            assert n_blocks <= LANES
            gate = jnp.zeros((rows, LANES), F32)
            for gg in range(n_groups):
                for t in range(blocks_per_step):
                    blk = z_s[gg, :, t * MOBA_BLOCK:(t + 1) * MOBA_BLOCK]
                    gate = jnp.where(lane == gg * blocks_per_step + t,
                                     jnp.sum(blk, axis=1, keepdims=True), gate)
            gm = jnp.where(lane < n_blocks, gate, -jnp.inf)
            sel = jnp.zeros((rows, LANES), F32)
            for _ in range(min(MOBA_TOPK, n_blocks)):
                mx = jnp.max(gm, axis=1, keepdims=True)
                idx = jnp.min(jnp.where(gm == mx, lane, LANES), axis=1, keepdims=True)
                pick = lane == idx
                sel = jnp.where(pick, 1.0, sel)
                gm = jnp.where(pick, -jnp.inf, gm)
            sel_b = sel.astype(BF16)
            own = (lane <= row_i) & (lane < dec_seq)
            zn = jnp.where(own, zn, NEG_BIG)
            mx = jnp.max(zn, axis=1, keepdims=True)
            masked = []
            for gg in range(n_groups):
                bl = lax.broadcasted_iota(jnp.int32, (LANES, STEP_KEYS), 0)
                bk = lax.broadcasted_iota(jnp.int32, (LANES, STEP_KEYS), 1)
                expand = jnp.where(bl == gg * blocks_per_step + bk // MOBA_BLOCK, 1.0, 0.0).astype(BF16)
                chosen = jnp.dot(sel_b, expand, preferred_element_type=F32) > 0.5
                s = jnp.where(chosen, z_s[gg], NEG_BIG)
                masked.append(s)
                mx = jnp.maximum(mx, jnp.max(s, axis=1, keepdims=True))
            p_new = jnp.exp2(zn - mx)
            denom = jnp.sum(p_new, axis=1, keepdims=True)
            anew_s[...] = p_new.astype(BF16)
            for gg in range(n_groups):
                pr = jnp.exp2(masked[gg] - mx)
                denom = denom + jnp.sum(pr, axis=1, keepdims=True)
                a_s[gg] = pr.astype(BF16)
            norm_s[...] = denom

    def write_out():
        o = o_acc[...] / norm_s[...]
        hd = lax.broadcasted_iota(jnp.int32, (dec_seq, W_MIX), 1) // HEAD_DIM
        out = jnp.zeros((dec_seq, W_MIX), F32)
        for h in range(N_HEADS):
            out = jnp.where(hd == h, o[h * dec_seq:(h + 1) * dec_seq, :], out)
        o_ref[...] = out.astype(o_ref.dtype)

    def make_step(step):
        def run():
            ahead = step + lookahead
            if ahead < n_steps:
                start_step(ahead, b)
            else:
                @pl.when(b + 1 < n_batch)
                def _start_next_sequence():
                    start_step(ahead - n_steps, b + 1)
            wait_step(step)
            if step == 0:
                block_diagonal_queries()
            if step < n_groups:
                z_s[step] = jnp.dot(qbd[...], step_pages(step), preferred_element_type=F32)
                if step == n_groups - 1:
                    weights()
                    vn = jnp.concatenate([vn_ref[...], jnp.zeros((LANES - dec_seq, W_MIX), F32)], axis=0)
                    o_acc[...] = jnp.dot(anew_s[...], vn.astype(BF16), preferred_element_type=F32)
            else:
                o_acc[...] = o_acc[...] + lax.dot_general(
                    a_s[step - n_groups], step_pages(step), NT_DIMS, preferred_element_type=F32)
            if step == n_steps - 1:
                write_out()
        return run

    return prime, [make_step(step) for step in range(n_steps)]


def _run_with_rider(work, cfg, pt_ref, rider_in, rider_out, rider_scratch):
    if cfg is None:
        for item in work:
            item()
        return
    prime, steps = _decode_steps(cfg, pl.program_id(0), pt_ref, *rider_in, *rider_out, rider_scratch)
    prime()
    done = 0
    for i, item in enumerate(work):
        item()
        upto = len(steps) * (i + 1) // len(work)
        for step in steps[done:upto]:
            step()
        done = upto


def _call_with_rider(body, rider, *, grid, in_specs, args, out_specs, out_shape, scratch_shapes, name):
    params = pltpu.CompilerParams(dimension_semantics=("arbitrary",) * len(grid),
                                  vmem_limit_bytes=VMEM_LIMIT)
    if rider is None:
        return pl.pallas_call(
            functools.partial(body, rider=None), grid=grid, in_specs=in_specs, out_specs=out_specs,
            out_shape=out_shape, scratch_shapes=scratch_shapes, compiler_params=params, name=name,
        )(*args)
    dec_batch, dec_seq, _ = rider.q.shape
    assert grid == (dec_batch,), "a rider needs one host grid step per decode sequence"
    cfg = _RiderConfig(rider.mode, rider.n_pages, dec_seq, dec_batch, rider.ring_steps)
    small = pl.BlockSpec((None, dec_seq, W_MIX), lambda i, *_: (i, 0, 0))
    hbm = pl.BlockSpec(memory_space=pl.ANY)
    grid_spec = pltpu.PrefetchScalarGridSpec(
        num_scalar_prefetch=1, grid=grid,
        in_specs=list(in_specs) + [small, small, small, hbm, hbm],
        out_specs=list(out_specs) + [small],
        scratch_shapes=list(scratch_shapes) + _decode_scratch(cfg))
    return pl.pallas_call(
        functools.partial(body, rider=cfg), grid_spec=grid_spec,
        out_shape=list(out_shape) + [jax.ShapeDtypeStruct((dec_batch, dec_seq, W_MIX), BF16)],
        compiler_params=params, name=name + "_decode_" + rider.mode,
    )(rider.page_table, *args, rider.q, rider.k_new, rider.v_new, rider.cache_k, rider.cache_v)


def _rope_tables(pos):
    inv = ROPE_THETA ** (-jnp.arange(HALF_DIM, dtype=F32) / HALF_DIM)
    ang = pos.astype(F32)[:, None] * inv[None, :]
    cos, sin = jnp.cos(ang), jnp.sin(ang)
    cos_lane = jnp.tile(cos, (1, LANES // HALF_DIM))
    sin_lane = jnp.tile(jnp.concatenate([-sin, sin], axis=1), (1, LANES // HEAD_DIM))
    return cos_lane, sin_lane, cos.T, sin.T


def kernel(x_prompt, x_sample, cache_moba_k, cache_moba_v, cache_sb_k, cache_sb_v, page_table,
           g_mix, w_in, w_branch_moba, w_branch_sb, w_out, g_ffn, w_ffn_gate, w_ffn_up,
           w_ffn_down, g_final):
    batch, seq, d = x_prompt.shape
    dec_batch, dec_seq, _ = x_sample.shape
    depth = w_in.shape[0]
    n_pages = page_table.shape[1]
    past_len = n_pages * PAGE_SIZE
    n_pool = cache_moba_k.shape[1]
    assert seq % MOBA_BLOCK == 0 and past_len % MOBA_BLOCK == 0 and n_pages % PAGES_PER_STEP == 0
    assert (batch * seq) % dec_batch == 0 and seq % ((batch * seq) // dec_batch) == 0
    assert depth == 1

    cos_p, sin_p, cos_pt, sin_pt = _rope_tables(jnp.arange(seq, dtype=jnp.int32))
    cos_p, sin_p = jnp.tile(cos_p, (batch, 1)), jnp.tile(sin_p, (batch, 1))
    cos_s, sin_s, _, _ = _rope_tables(past_len + jnp.arange(dec_seq, dtype=jnp.int32))
    cos_s, sin_s = jnp.tile(cos_s, (dec_batch, 1)), jnp.tile(sin_s, (dec_batch, 1))
    pt_flat = page_table.reshape(-1)

    xp = x_prompt.reshape(batch * seq, d)
    xs = x_sample.reshape(dec_batch * dec_seq, d)
    m_s = dec_batch * dec_seq
    g_last = g_final.reshape(1, d)
    l = 0
    w_in_b = w_in[l].astype(BF16)
    wq = jnp.concatenate([w_in_b[:, 0:W_MIX], w_in_b[:, 3 * W_MIX:4 * W_MIX]], axis=1)
    wkv = jnp.concatenate([w_in_b[:, W_MIX:3 * W_MIX], w_in_b[:, 4 * W_MIX:6 * W_MIX]], axis=1)
    wgt = w_in_b[:, 6 * W_MIX:]
    bf = lambda w: w[l].astype(BF16)
    wa, ws, wo = bf(w_branch_moba), bf(w_branch_sb), bf(w_out)
    wg, wu, wd = bf(w_ffn_gate), bf(w_ffn_up), bf(w_ffn_down)
    gm, gf = g_mix[l].reshape(1, d), g_ffn[l].reshape(1, d)

    ka2, va2, ks2, vs2, qa2, qs2, ga2, gs2 = _proj(xs, gm, wq, wkv, wgt, cos_s, sin_s, bm=m_s)
    r3 = lambda t: t.reshape(dec_batch, dec_seq, W_MIX)
    pool = lambda cch: cch[l].transpose(0, 2, 3, 1).reshape(n_pool, W_MIX, PAGE_SIZE)
    steps_per_seq = 2 * n_pages // PAGES_PER_STEP
    moba_rider = _Rider("moba", pt_flat, r3(qa2), r3(ka2), r3(va2), pool(cache_moba_k),
                        pool(cache_moba_v), n_pages, math.gcd(PROJ_RING_STEPS, steps_per_seq))
    sb_rider = _Rider("sb", pt_flat, r3(qs2), r3(ks2), r3(vs2), pool(cache_sb_k),
                      pool(cache_sb_v), n_pages, math.gcd(POST_RING_STEPS, steps_per_seq))

    bm = (batch * seq) // dec_batch
    ka_t, va_t, ks_t, vs_t, qa, qs, ga, gs, oa2 = _proj(
        xp, gm, wq, wkv.T, wgt, cos_p, sin_p, cos_pt, sin_pt, bm=bm, seq=seq, rider=moba_rider)
    oa = _moba_prompt(qa, ka_t, va_t, batch=batch, seq=seq)
    os_ = _sb_prompt(qs, ks_t, vs_t, batch=batch, seq=seq)
    y_prompt, os2 = _post(xp, oa, os_, ga, gs, wa, ws, wo, gf, wg, wu, wd, g_last, bm=bm,
                          rider=sb_rider)

    (y_sample,) = _post(xs, oa2.reshape(m_s, W_MIX), os2.reshape(m_s, W_MIX), ga2, gs2,
                        wa, ws, wo, gf, wg, wu, wd, g_last, bm=m_s)

    rows_p = lambda t: t.reshape(1, batch, N_HEADS, HEAD_DIM, seq).transpose(0, 1, 4, 2, 3)
    rows_s = lambda t: t.reshape(1, dec_batch, dec_seq, N_HEADS, HEAD_DIM)
    return (y_prompt.reshape(batch, seq, d), y_sample.reshape(dec_batch, dec_seq, d),
            rows_p(ka_t), rows_p(va_t), rows_p(ks_t), rows_p(vs_t),
            rows_s(ka2), rows_s(va2), rows_s(ks2), rows_s(vs2))
```

```python
import functools
import itertools
import math
from typing import NamedTuple

import jax
import jax.numpy as jnp
from jax import lax
from jax.experimental import pallas as pl
from jax.experimental.pallas import tpu as pltpu

F32 = jnp.float32
BF16 = jnp.bfloat16

HEAD_DIM = 64
HALF_DIM = HEAD_DIM // 2
N_HEADS = 8
W_MIX = N_HEADS * HEAD_DIM
LANES = 128
N_PAIRS = W_MIX // LANES
MOBA_BLOCK = 256
MOBA_TOPK = 3
Q_BLOCK = 128
PAGE_SIZE = 128
ROPE_THETA = 10000.0
RMS_EPS = 1e-6
NEG_BIG = -1e30
SCORE_SCALE = math.log2(math.e) / math.sqrt(HEAD_DIM)
NT_DIMS = (((1,), (1,)), ((), ()))
VMEM_LIMIT = 56 * 1024 * 1024


def _resident(shape):
    return pl.BlockSpec(shape, lambda *_: (0,) * len(shape), pipeline_mode=pl.Buffered(1))


def _log2_sigmoid(z2):
    return jnp.minimum(z2, 0.0) - jnp.log2(1.0 + jnp.exp2(-jnp.abs(z2)))


def _split_bf16(x):
    hi = x.astype(BF16)
    lo = (x - hi.astype(F32)).astype(BF16)
    return hi, lo


def _rms_norm_bf16(x, g):
    ms = jnp.mean(x * x, axis=-1, keepdims=True)
    return (x * lax.rsqrt(ms + RMS_EPS) * g).astype(BF16)


def _proj_body(*refs, kv_transposed, rider):
    pt_ref, refs = (refs[0], refs[1:]) if rider else (None, refs)
    x_ref, g_ref, wq_ref, wkv_ref, wg_ref, cos_ref, sin_ref = refs[:7]
    refs = refs[7:]
    if kv_transposed:
        cos_t_ref, sin_t_ref = refs[:2]
        refs = refs[2:]
    rider_in, refs = (refs[:5], refs[5:]) if rider else (None, refs)
    ka_ref, va_ref, ks_ref, vs_ref, qa_ref, qs_ref, ga_ref, gs_ref = refs[:8]
    refs = refs[8:]
    rider_out, refs = (refs[:1], refs[1:]) if rider else ((), refs)
    h_s, rider_scratch = refs[0], refs[1:]
    d_model = ga_ref.shape[1]

    h_s[...] = _rms_norm_bf16(x_ref[...], g_ref[...])
    cos = cos_ref[...]
    sin = sin_ref[...]
    lane = lax.broadcasted_iota(jnp.int32, cos.shape, 1)
    first_half = (lane % HEAD_DIM) < HALF_DIM

    def rope_slab(xs):
        rot = jnp.where(first_half, pltpu.roll(xs, LANES - HALF_DIM, 1), pltpu.roll(xs, HALF_DIM, 1))
        return xs * cos + rot * sin

    def queries():
        q = jnp.dot(h_s[...], wq_ref[...], preferred_element_type=F32)
        for p in range(N_PAIRS):
            sl = slice(p * LANES, (p + 1) * LANES)
            qa_ref[:, sl] = (rope_slab(q[:, sl]) * SCORE_SCALE).astype(qa_ref.dtype)
        qs_ref[...] = (q[:, W_MIX:] * SCORE_SCALE).astype(qs_ref.dtype)

    def kv_segment(i):
        if kv_transposed:
            return lax.dot_general(wkv_ref[i * W_MIX:(i + 1) * W_MIX, :], h_s[...], NT_DIMS,
                                   preferred_element_type=F32)
        return jnp.dot(h_s[...], wkv_ref[:, i * W_MIX:(i + 1) * W_MIX], preferred_element_type=F32)

    def moba_keys():
        ka = kv_segment(0)
        if kv_transposed:
            cos_t = cos_t_ref[...]
            sin_t = sin_t_ref[...]
            for hd in range(N_HEADS):
                x1 = ka[hd * HEAD_DIM:hd * HEAD_DIM + HALF_DIM, :]
                x2 = ka[hd * HEAD_DIM + HALF_DIM:(hd + 1) * HEAD_DIM, :]
                ka_ref[hd * HEAD_DIM:hd * HEAD_DIM + HALF_DIM, :] = x1 * cos_t - x2 * sin_t
                ka_ref[hd * HEAD_DIM + HALF_DIM:(hd + 1) * HEAD_DIM, :] = x2 * cos_t + x1 * sin_t
        else:
            for p in range(N_PAIRS):
                sl = slice(p * LANES, (p + 1) * LANES)
                ka_ref[:, sl] = rope_slab(ka[:, sl])

    def plain_segment(i, out_ref):
        def run():
            out_ref[...] = kv_segment(i)
        return run

    def gate(lo, out_ref):
        def run():
            g = jnp.dot(h_s[...], wg_ref[:, lo:lo + d_model], preferred_element_type=F32)
            out_ref[...] = jax.nn.sigmoid(g).astype(out_ref.dtype)
        return run

    work = [queries, moba_keys, plain_segment(1, va_ref), plain_segment(2, ks_ref),
            plain_segment(3, vs_ref), gate(0, ga_ref), gate(d_model, gs_ref)]
    _run_with_rider(work, rider, pt_ref, rider_in, rider_out, rider_scratch)


def _proj(x, g_mix, wq, wkv, wg, cos, sin, cos_t=None, sin_t=None, *, bm, seq=None, rider=None):
    m, d = x.shape
    kv_transposed = seq is not None
    row = lambda width: pl.BlockSpec((bm, width), lambda i, *_: (i, 0))
    in_specs = [row(d), _resident((1, d)), _resident(wq.shape), _resident(wkv.shape),
                _resident(wg.shape), row(LANES), row(LANES)]
    args = [x, g_mix, wq, wkv, wg, cos, sin]
    if kv_transposed:
        per_seq = seq // bm
        tab = pl.BlockSpec((HALF_DIM, bm), lambda i, *_: (0, i % per_seq))
        in_specs += [tab, tab]
        args += [cos_t, sin_t]
        kv_shape = jax.ShapeDtypeStruct((m // seq, W_MIX, seq), F32)
        kv_spec = pl.BlockSpec((None, W_MIX, bm), lambda i, *_: (i // per_seq, 0, i % per_seq))
    else:
        kv_shape = jax.ShapeDtypeStruct((m, W_MIX), F32)
        kv_spec = row(W_MIX)
    out_specs = [kv_spec] * 4 + [row(W_MIX)] * 2 + [row(d)] * 2
    out_shape = [kv_shape] * 4 + [jax.ShapeDtypeStruct((m, W_MIX), BF16)] * 2 \
        + [jax.ShapeDtypeStruct((m, d), BF16)] * 2
    return _call_with_rider(
        functools.partial(_proj_body, kv_transposed=kv_transposed), rider,
        grid=(m // bm,), in_specs=in_specs, args=args, out_specs=out_specs, out_shape=out_shape,
        scratch_shapes=[pltpu.VMEM((bm, d), BF16)],
        name="proj_prompt" if kv_transposed else "proj_decode")


FF_CHUNK = 256


def _post_body(*refs, rider):
    pt_ref, refs = (refs[0], refs[1:]) if rider else (None, refs)
    (x_ref, oa_ref, os_ref, ga_ref, gs_ref, wa_ref, ws_ref, wo_ref,
     gf_ref, wg_ref, wu_ref, wd_ref, gfin_ref) = refs[:13]
    refs = refs[13:]
    rider_in, refs = (refs[:5], refs[5:]) if rider else (None, refs)
    y_ref, refs = refs[0], refs[1:]
    rider_out, refs = (refs[:1], refs[1:]) if rider else ((), refs)
    h_s, acc_s, rider_scratch = refs[0], refs[1], refs[2:]
    d_ff = wg_ref.shape[1]

    def merge_branches():
        merged = (ga_ref[...].astype(F32) * jnp.dot(oa_ref[...], wa_ref[...], preferred_element_type=F32)
                  + gs_ref[...].astype(F32) * jnp.dot(os_ref[...], ws_ref[...], preferred_element_type=F32))
        h_s[...] = merged.astype(BF16)

    def out_projection():
        x1 = x_ref[...] + jnp.dot(h_s[...], wo_ref[...], preferred_element_type=F32)
        acc_s[...] = x1
        h_s[...] = _rms_norm_bf16(x1, gf_ref[...])

    def ffn_columns(lo):
        def run():
            h2 = h_s[...]
            gate = jnp.dot(h2, wg_ref[:, lo:lo + FF_CHUNK], preferred_element_type=F32)
            up = jnp.dot(h2, wu_ref[:, lo:lo + FF_CHUNK], preferred_element_type=F32)
            ff = (gate * jax.nn.sigmoid(gate) * up).astype(BF16)
            acc_s[...] += jnp.dot(ff, wd_ref[lo:lo + FF_CHUNK, :], preferred_element_type=F32)
        return run

    def closing_norm():
        x2 = acc_s[...]
        ms2 = jnp.mean(x2 * x2, axis=-1, keepdims=True)
        y_ref[...] = x2 * lax.rsqrt(ms2 + RMS_EPS) * gfin_ref[...]

    work = [merge_branches, out_projection] + [ffn_columns(lo) for lo in range(0, d_ff, FF_CHUNK)] \
        + [closing_norm]
    _run_with_rider(work, rider, pt_ref, rider_in, rider_out, rider_scratch)


def _post(x, oa, os_, ga, gs, wa, ws, wo, g_ffn, wg, wu, wd, g_final, *, bm, rider=None):
    m, d = x.shape
    d_ff = wg.shape[1]
    assert d_ff % FF_CHUNK == 0
    row = lambda width: pl.BlockSpec((bm, width), lambda i, *_: (i, 0))
    return _call_with_rider(
        _post_body, rider, grid=(m // bm,),
        in_specs=[row(d), row(W_MIX), row(W_MIX), row(d), row(d),
                  _resident((W_MIX, d)), _resident((W_MIX, d)), _resident((d, d)),
                  _resident((1, d)), _resident((d, d_ff)), _resident((d, d_ff)),
                  _resident((d_ff, d)), _resident((1, d))],
        args=[x, oa, os_, ga, gs, wa, ws, wo, g_ffn, wg, wu, wd, g_final],
        out_specs=[row(d)], out_shape=[jax.ShapeDtypeStruct((m, d), F32)],
        scratch_shapes=[pltpu.VMEM((bm, d), BF16), pltpu.VMEM((bm, d), F32)],
        name="post_ffn")


GATE_GROUP = 8
HEADS_PER_STAGE = 8


def _moba_parts(c, q_ref, k_ref, v_ref, o_ref, kaug, vb, kmhi, kmlo, lhs, m_s, acc_s, *, seq):
    nb = seq // MOBA_BLOCK
    assert nb <= GATE_GROUP and N_HEADS * GATE_GROUP <= LANES

    @pl.when(c == 0)
    def _per_sequence_setup():
        r = lax.broadcasted_iota(jnp.int32, (LANES, MOBA_BLOCK), 0)
        lane_w = lax.broadcasted_iota(jnp.int32, (W_MIX, LANES), 1)
        row_w = lax.broadcasted_iota(jnp.int32, (W_MIX, LANES), 0)
        km = jnp.zeros((W_MIX, LANES), F32)
        for t in range(nb):
            sl = slice(t * MOBA_BLOCK, (t + 1) * MOBA_BLOCK)
            ind = jnp.where((r < N_HEADS * GATE_GROUP) & ((r % GATE_GROUP) == t), 1.0, 0.0).astype(BF16)
            for p in range(N_PAIRS):
                kaug[p, t, 0:LANES, :] = k_ref[p * LANES:(p + 1) * LANES, sl].astype(BF16)
                kaug[p, t, LANES:2 * LANES, :] = ind
                vb[p, t, 0:LANES, :] = v_ref[p * LANES:(p + 1) * LANES, sl].astype(BF16)
                vb[p, t, LANES:2 * LANES, :] = jnp.ones((LANES, MOBA_BLOCK), BF16)
            block_sum = jnp.sum(k_ref[:, sl], axis=1, keepdims=True)
            km = jnp.where((lane_w % GATE_GROUP) == t, block_sum, km)
        km = jnp.where((lane_w < N_HEADS * GATE_GROUP) & ((row_w // HEAD_DIM) == (lane_w // GATE_GROUP)),
                       km * (1.0 / MOBA_BLOCK), 0.0)
        hi, lo = _split_bf16(km)
        kmhi[...] = hi
        kmlo[...] = lo

    q = q_ref[...]
    gate = (jnp.dot(q, kmhi[...], preferred_element_type=F32)
            + jnp.dot(q, kmlo[...], preferred_element_type=F32))
    lane = lax.broadcasted_iota(jnp.int32, (Q_BLOCK, LANES), 1)
    n = lane % GATE_GROUP
    cur = (c * Q_BLOCK) // MOBA_BLOCK
    g = jnp.where(n < cur, gate, -jnp.inf)
    rank = jnp.zeros((Q_BLOCK, LANES), F32)
    for r in range(1, GATE_GROUP):
        wraps = (n + r) >= GATE_GROUP
        other = jnp.where(wraps, pltpu.roll(g, GATE_GROUP - r, 1), pltpu.roll(g, LANES - r, 1))
        beats = (other > g) | (wraps & (other == g))
        rank = rank + beats.astype(F32)
    keep = ((n < cur) & (rank < MOBA_TOPK)) | (n >= cur)
    bias = jnp.where(keep | (lane >= N_HEADS * GATE_GROUP), 0.0, NEG_BIG)

    qf = q.astype(F32)
    for h in range(N_HEADS):
        p, j = divmod(h, 2)
        in_head = (lane >= j * HEAD_DIM) & (lane < (j + 1) * HEAD_DIM)
        lhs[h, :, 0:LANES] = jnp.where(in_head, qf[:, p * LANES:(p + 1) * LANES], 0.0).astype(BF16)
        in_group = (lane >= h * GATE_GROUP) & (lane < (h + 1) * GATE_GROUP)
        lhs[h, :, LANES:2 * LANES] = jnp.where(in_group, bias, 0.0).astype(BF16)
    m_s[...] = jnp.full(m_s.shape, NEG_BIG, F32)
    acc_s[...] = jnp.zeros(acc_s.shape, F32)

    qq = lax.broadcasted_iota(jnp.int32, (Q_BLOCK, MOBA_BLOCK), 0)
    kk = lax.broadcasted_iota(jnp.int32, (Q_BLOCK, MOBA_BLOCK), 1)
    causal = kk <= qq + (c * Q_BLOCK - cur * MOBA_BLOCK)
    twice = lambda x: jnp.concatenate([x, x], axis=1)

    def tile_group(t0, n_tiles, own_last):
        tiles = range(n_tiles)
        for h0 in range(0, N_HEADS, HEADS_PER_STAGE):
            heads = range(h0, h0 + HEADS_PER_STAGE)
            s = {h: [jnp.dot(lhs[h], kaug[h // 2, t0 + i], preferred_element_type=F32)
                     for i in tiles] for h in heads}
            yield
            if own_last:
                for h in heads:
                    s[h][-1] = jnp.where(causal, s[h][-1], NEG_BIG)
            m_old = {h: m_s[h] for h in heads}
            m_new = {}
            for h in heads:
                top = functools.reduce(jnp.maximum, s[h])
                m_new[h] = jnp.maximum(m_old[h], jnp.max(top, axis=1, keepdims=True))
            alpha = {h: jnp.exp2(m_old[h] - m_new[h]) for h in heads}
            yield
            pr = {h: [jnp.exp2(s[h][i] - twice(m_new[h])).astype(BF16) for i in tiles]
                  for h in heads}
            yield
            for h in heads:
                pv = sum(lax.dot_general(pr[h][i], vb[h // 2, t0 + i], NT_DIMS,
                                         preferred_element_type=F32) for i in tiles)
                acc_s[h] = twice(alpha[h]) * acc_s[h] + pv
                m_s[h] = m_new[h]
            yield

    def finish():
        for p in range(N_PAIRS):
            acc_a, acc_b = acc_s[2 * p], acc_s[2 * p + 1]
            oa = acc_a[:, :LANES] / acc_a[:, LANES:]
            ob = acc_b[:, :LANES] / acc_b[:, LANES:]
            o_ref[:, p * LANES:(p + 1) * LANES] = jnp.where(lane < HEAD_DIM, oa, ob).astype(o_ref.dtype)

    return tile_group, finish, cur


def _moba_scratch(seq):
    nb = seq // MOBA_BLOCK
    return [pltpu.VMEM((N_PAIRS, nb, 2 * LANES, MOBA_BLOCK), BF16),
            pltpu.VMEM((N_PAIRS, nb, 2 * LANES, MOBA_BLOCK), BF16),
            pltpu.VMEM((W_MIX, LANES), BF16),
            pltpu.VMEM((W_MIX, LANES), BF16),
            pltpu.VMEM((N_HEADS, Q_BLOCK, 2 * LANES), BF16),
            pltpu.VMEM((N_HEADS, Q_BLOCK, LANES), F32),
            pltpu.VMEM((N_HEADS, Q_BLOCK, 2 * LANES), F32)]


SB_TILE = 256


def _suffix_sum_matrix(n):
    r = lax.broadcasted_iota(jnp.int32, (n, n), 0)
    c = lax.broadcasted_iota(jnp.int32, (n, n), 1)
    return jnp.where(r > c, 1.0, 0.0).astype(BF16)


def _sb_log_terms(z2, past, tri):
    ls = _log2_sigmoid(z2)
    lk = ls - z2
    if past is not None:
        lk = jnp.where(past, lk, 0.0)
    tail = jnp.dot(lk.astype(BF16), tri, preferred_element_type=F32)
    return ls, lk, tail


def _sb_weights(z2, past, later, tri):
    ls, lk, tail = _sb_log_terms(z2, past, tri)
    a = jnp.exp2(ls + tail + later)
    if past is not None:
        a = jnp.where(past, a, 0.0)
    return a, jnp.sum(lk, axis=1, keepdims=True)


def _sb_parts(c, q_ref, k_ref, v_ref, o_ref, kb, vb, tri, later_s, acc_s, *, seq):
    @pl.when(c == 0)
    def _per_sequence_setup():
        for t in range(seq // SB_TILE):
            sl = slice(t * SB_TILE, (t + 1) * SB_TILE)
            for p in range(N_PAIRS):
                kb[p, t] = k_ref[p * LANES:(p + 1) * LANES, sl].astype(BF16)
                vb[p, t] = v_ref[p * LANES:(p + 1) * LANES, sl].astype(BF16)
        tri[...] = _suffix_sum_matrix(SB_TILE)

    qf = q_ref[...].astype(F32)
    lane = lax.broadcasted_iota(jnp.int32, (Q_BLOCK, LANES), 1)
    qm = []
    for h in range(N_HEADS):
        p, j = divmod(h, 2)
        in_head = (lane >= j * HEAD_DIM) & (lane < (j + 1) * HEAD_DIM)
        qm.append(jnp.where(in_head, qf[:, p * LANES:(p + 1) * LANES], 0.0).astype(BF16))
    later_s[...] = jnp.zeros(later_s.shape, F32)
    acc_s[...] = jnp.zeros(acc_s.shape, F32)

    diag = (c * Q_BLOCK) // SB_TILE
    qq = lax.broadcasted_iota(jnp.int32, (Q_BLOCK, SB_TILE), 0)
    kk = lax.broadcasted_iota(jnp.int32, (Q_BLOCK, SB_TILE), 1)
    past_diag = kk < qq + (c * Q_BLOCK - diag * SB_TILE)

    def tile_group(t0, n_tiles, diag_last):
        tiles = range(n_tiles)
        masked = lambda i: diag_last and i == n_tiles - 1
        u = tri[...]
        for h0 in range(0, N_HEADS, HEADS_PER_STAGE):
            heads = range(h0, h0 + HEADS_PER_STAGE)
            z = {h: [jnp.dot(qm[h], kb[h // 2, t0 + i], preferred_element_type=F32) for i in tiles]
                 for h in heads}
            yield
            ls = {h: [_log2_sigmoid(z[h][i]) for i in tiles] for h in heads}
            lk = {h: [ls[h][i] - z[h][i] for i in tiles] for h in heads}
            lk = {h: [jnp.where(past_diag, lk[h][i], 0.0) if masked(i) else lk[h][i]
                      for i in tiles] for h in heads}
            yield
            tail = {h: [jnp.dot(lk[h][i].astype(BF16), u, preferred_element_type=F32)
                        for i in tiles] for h in heads}
            yield
            a = {h: [jnp.exp2(ls[h][i] + tail[h][i]) for i in tiles] for h in heads}
            a = {h: [jnp.where(past_diag, a[h][i], 0.0) if masked(i) else a[h][i] for i in tiles]
                 for h in heads}
            yield
            for h in heads:
                later = later_s[h]
                out = acc_s[h]
                for i in reversed(tiles):
                    pv = lax.dot_general(a[h][i].astype(BF16), vb[h // 2, t0 + i], NT_DIMS,
                                         preferred_element_type=F32)
                    out = out + jnp.exp2(later) * pv
                    later = later + jnp.sum(lk[h][i], axis=1, keepdims=True)
                acc_s[h] = out
                later_s[h] = later
            yield

    def finish():
        for p in range(N_PAIRS):
            o_ref[:, p * LANES:(p + 1) * LANES] = jnp.where(
                lane < HEAD_DIM, acc_s[2 * p], acc_s[2 * p + 1]).astype(o_ref.dtype)

    return tile_group, finish, diag


def _sb_scratch(seq):
    nt = seq // SB_TILE
    return [pltpu.VMEM((N_PAIRS, nt, LANES, SB_TILE), BF16),
            pltpu.VMEM((N_PAIRS, nt, LANES, SB_TILE), BF16),
            pltpu.VMEM((SB_TILE, SB_TILE), BF16),
            pltpu.VMEM((N_HEADS, Q_BLOCK, LANES), F32),
            pltpu.VMEM((N_HEADS, Q_BLOCK, LANES), F32)]


def _mix_prompt_body(qa_ref, ka_ref, va_ref, qs_ref, ks_ref, vs_ref, oa_ref, os_ref, *scratch, seq):
    c = pl.program_id(1)
    n_moba = len(_moba_scratch(seq))
    moba_group, moba_finish, own = _moba_parts(c, qa_ref, ka_ref, va_ref, oa_ref,
                                               *scratch[:n_moba], seq=seq)
    sb_group, sb_finish, diag = _sb_parts(c, qs_ref, ks_ref, vs_ref, os_ref,
                                          *scratch[n_moba:], seq=seq)
    assert MOBA_BLOCK == SB_TILE

    def together(*generators):
        for _ in itertools.zip_longest(*generators):
            pass

    @pl.when(own % 2 == 0)
    def _last_tile_alone():
        together(moba_group(own, 1, True), sb_group(diag, 1, True))

    @pl.when(own % 2 == 1)
    def _last_two_tiles():
        together(moba_group(own - 1, 2, True), sb_group(diag - 1, 2, True))

    n_pairs = own // 2

    def earlier_pair(i, carry):
        together(moba_group(2 * i, 2, False), sb_group(2 * (n_pairs - 1 - i), 2, False))
        return carry

    lax.fori_loop(0, n_pairs, earlier_pair, 0)
    moba_finish()
    sb_finish()


def _mix_prompt(qa, ka_t, va_t, qs, ks_t, vs_t, *, batch, seq):
    nq = seq // Q_BLOCK
    q_spec = pl.BlockSpec((Q_BLOCK, W_MIX), lambda b, c: (b * nq + c, 0))
    kv_spec = pl.BlockSpec((None, W_MIX, seq), lambda b, c: (b, 0, 0))
    out = jax.ShapeDtypeStruct((batch * seq, W_MIX), BF16)
    return pl.pallas_call(
        functools.partial(_mix_prompt_body, seq=seq),
        grid=(batch, nq),
        in_specs=[q_spec, kv_spec, kv_spec, q_spec, kv_spec, kv_spec],
        out_specs=[q_spec, q_spec],
        out_shape=[out, out],
        scratch_shapes=_moba_scratch(seq) + _sb_scratch(seq),
        compiler_params=pltpu.CompilerParams(
            dimension_semantics=("arbitrary", "arbitrary"), vmem_limit_bytes=VMEM_LIMIT),
        name="mix_prompt",
    )(qa, ka_t, va_t, qs, ks_t, vs_t)


PAGES_PER_STEP = 8
STEP_KEYS = PAGES_PER_STEP * PAGE_SIZE
PROJ_RING_STEPS = 8
POST_RING_STEPS = 4


class _Rider(NamedTuple):
    mode: str
    page_table: jax.Array
    q: jax.Array
    k_new: jax.Array
    v_new: jax.Array
    cache_k: jax.Array
    cache_v: jax.Array
    n_pages: int
    ring_steps: int


class _RiderConfig(NamedTuple):
    mode: str
    n_pages: int
    dec_seq: int
    n_batch: int
    ring_steps: int


def _decode_scratch(cfg):
    npg = PAGES_PER_STEP
    n_groups = cfg.n_pages // npg
    rows = N_HEADS * cfg.dec_seq
    return [pltpu.VMEM((cfg.ring_steps * npg, W_MIX, PAGE_SIZE), F32),
            pltpu.SemaphoreType.DMA((cfg.ring_steps,)),
            pltpu.VMEM((rows, W_MIX), BF16),
            pltpu.VMEM((n_groups, rows, STEP_KEYS), F32),
            pltpu.VMEM((n_groups, rows, STEP_KEYS), BF16),
            pltpu.VMEM((rows, LANES), BF16),
            pltpu.VMEM((rows, 1), F32),
            pltpu.VMEM((rows, W_MIX), F32)]


def _decode_steps(cfg, b, pt_ref, q_ref, kn_ref, vn_ref, ck_hbm, cv_hbm, o_ref, scratch):
    pages, sems, qbd, z_s, a_s, anew_s, norm_s, o_acc = scratch
    mode, n_pages, dec_seq, n_batch, ring_steps = cfg
    npg = PAGES_PER_STEP
    n_groups = n_pages // npg
    n_steps = 2 * n_groups
    lookahead = ring_steps - 1
    assert n_steps % ring_steps == 0
    rows = N_HEADS * dec_seq
    assert rows <= LANES and dec_seq <= 8

    def page_copy(step, j, batch):
        src = ck_hbm if step < n_groups else cv_hbm
        page = (step % n_groups) * npg + j
        ring = step % ring_steps
        return pltpu.make_async_copy(src.at[pt_ref[batch * n_pages + page]],
                                     pages.at[ring * npg + j], sems.at[ring])

    def start_step(step, batch):
        for j in range(npg):
            page_copy(step, j, batch).start()

    def wait_step(step):
        for j in range(npg):
            page_copy(step, j, b).wait()

    def step_pages(step):
        ring = step % ring_steps
        return jnp.concatenate([pages[ring * npg + j].astype(BF16) for j in range(npg)], axis=1)

    row_i = lax.broadcasted_iota(jnp.int32, (rows, LANES), 0) % dec_seq
    lane = lax.broadcasted_iota(jnp.int32, (rows, LANES), 1)

    def prime():
        @pl.when(b == 0)
        def _prime_ring():
            for s in range(lookahead):
                start_step(s, 0)

    def block_diagonal_queries():
        qf = q_ref[...].astype(F32)
        qt = jnp.concatenate([qf] * N_HEADS, axis=0)
        r = lax.broadcasted_iota(jnp.int32, qt.shape, 0)
        l = lax.broadcasted_iota(jnp.int32, qt.shape, 1)
        qbd[...] = jnp.where((l // HEAD_DIM) == (r // dec_seq), qt, 0.0).astype(BF16)

    def weights():
        kn = jnp.concatenate([kn_ref[...], jnp.zeros((LANES - dec_seq, W_MIX), F32)], axis=0)
        zn = lax.dot_general(qbd[...], kn.astype(BF16), NT_DIMS, preferred_element_type=F32)
        if mode == "sb":
            tri = _suffix_sum_matrix(SB_TILE)
            past_new = lane < row_i
            a_new, later = _sb_weights(zn, past_new, 0.0, _suffix_sum_matrix(LANES))
            anew_s[...] = a_new.astype(BF16)
            for gg in range(n_groups - 1, -1, -1):
                for t in range(STEP_KEYS // SB_TILE - 1, -1, -1):
                    sl = slice(t * SB_TILE, (t + 1) * SB_TILE)
                    a, tot = _sb_weights(z_s[gg, :, sl], None, later, tri)
                    a_s[gg, :, sl] = a.astype(BF16)
                    later = later + tot
            norm_s[...] = jnp.ones(norm_s.shape, F32)
        else:
            blocks_per_step = STEP_KEYS // MOBA_BLOCK
            n_blocks = n_groups * blocks_per_step
            assert n_blocks <= LANES
            gate = jnp.zeros((rows, LANES), F32)
            for gg in range(n_groups):
                for t in range(blocks_per_step):
                    blk = z_s[gg, :, t * MOBA_BLOCK:(t + 1) * MOBA_BLOCK]
                    gate = jnp.where(lane == gg * blocks_per_step + t,
                                     jnp.sum(blk, axis=1, keepdims=True), gate)
            gm = jnp.where(lane < n_blocks, gate, -jnp.inf)
            sel = jnp.zeros((rows, LANES), F32)
            for _ in range(min(MOBA_TOPK, n_blocks)):
                mx = jnp.max(gm, axis=1, keepdims=True)
                idx = jnp.min(jnp.where(gm == mx, lane, LANES), axis=1, keepdims=True)
                pick = lane == idx
                sel = jnp.where(pick, 1.0, sel)
                gm = jnp.where(pick, -jnp.inf, gm)
            sel_b = sel.astype(BF16)
            own = (lane <= row_i) & (lane < dec_seq)
            zn = jnp.where(own, zn, NEG_BIG)
            mx = jnp.max(zn, axis=1, keepdims=True)
            masked = []
            for gg in range(n_groups):
                bl = lax.broadcasted_iota(jnp.int32, (LANES, STEP_KEYS), 0)
                bk = lax.broadcasted_iota(jnp.int32, (LANES, STEP_KEYS), 1)
                expand = jnp.where(bl == gg * blocks_per_step + bk // MOBA_BLOCK, 1.0, 0.0).astype(BF16)
                chosen = jnp.dot(sel_b, expand, preferred_element_type=F32) > 0.5
                s = jnp.where(chosen, z_s[gg], NEG_BIG)
                masked.append(s)
                mx = jnp.maximum(mx, jnp.max(s, axis=1, keepdims=True))
            p_new = jnp.exp2(zn - mx)
            denom = jnp.sum(p_new, axis=1, keepdims=True)
            anew_s[...] = p_new.astype(BF16)
            for gg in range(n_groups):
                pr = jnp.exp2(masked[gg] - mx)
                denom = denom + jnp.sum(pr, axis=1, keepdims=True)
                a_s[gg] = pr.astype(BF16)
            norm_s[...] = denom

    def write_out():
        o = o_acc[...] / norm_s[...]
        hd = lax.broadcasted_iota(jnp.int32, (dec_seq, W_MIX), 1) // HEAD_DIM
        out = jnp.zeros((dec_seq, W_MIX), F32)
        for h in range(N_HEADS):
            out = jnp.where(hd == h, o[h * dec_seq:(h + 1) * dec_seq, :], out)
        o_ref[...] = out.astype(o_ref.dtype)

    def make_step(step):
        def run():
            ahead = step + lookahead
            if ahead < n_steps:
                start_step(ahead, b)
            else:
                @pl.when(b + 1 < n_batch)
                def _start_next_sequence():
                    start_step(ahead - n_steps, b + 1)
            wait_step(step)
            if step == 0:
                block_diagonal_queries()
            if step < n_groups:
                z_s[step] = jnp.dot(qbd[...], step_pages(step), preferred_element_type=F32)
                if step == n_groups - 1:
                    weights()
                    vn = jnp.concatenate([vn_ref[...], jnp.zeros((LANES - dec_seq, W_MIX), F32)], axis=0)
                    o_acc[...] = jnp.dot(anew_s[...], vn.astype(BF16), preferred_element_type=F32)
            else:
                o_acc[...] = o_acc[...] + lax.dot_general(
                    a_s[step - n_groups], step_pages(step), NT_DIMS, preferred_element_type=F32)
            if step == n_steps - 1:
                write_out()
        return run

    return prime, [make_step(step) for step in range(n_steps)]


def _run_with_rider(work, cfg, pt_ref, rider_in, rider_out, rider_scratch):
    if cfg is None:
        for item in work:
            item()
        return
    prime, steps = _decode_steps(cfg, pl.program_id(0), pt_ref, *rider_in, *rider_out, rider_scratch)
    prime()
    done = 0
    for i, item in enumerate(work):
        item()
        upto = len(steps) * (i + 1) // len(work)
        for step in steps[done:upto]:
            step()
        done = upto


def _call_with_rider(body, rider, *, grid, in_specs, args, out_specs, out_shape, scratch_shapes, name):
    params = pltpu.CompilerParams(dimension_semantics=("arbitrary",) * len(grid),
                                  vmem_limit_bytes=VMEM_LIMIT)
    if rider is None:
        return pl.pallas_call(
            functools.partial(body, rider=None), grid=grid, in_specs=in_specs, out_specs=out_specs,
            out_shape=out_shape, scratch_shapes=scratch_shapes, compiler_params=params, name=name,
        )(*args)
    dec_batch, dec_seq, _ = rider.q.shape
    assert grid == (dec_batch,), "a rider needs one host grid step per decode sequence"
    cfg = _RiderConfig(rider.mode, rider.n_pages, dec_seq, dec_batch, rider.ring_steps)
    small = pl.BlockSpec((None, dec_seq, W_MIX), lambda i, *_: (i, 0, 0))
    hbm = pl.BlockSpec(memory_space=pl.ANY)
    grid_spec = pltpu.PrefetchScalarGridSpec(
        num_scalar_prefetch=1, grid=grid,
        in_specs=list(in_specs) + [small, small, small, hbm, hbm],
        out_specs=list(out_specs) + [small],
        scratch_shapes=list(scratch_shapes) + _decode_scratch(cfg))
    return pl.pallas_call(
        functools.partial(body, rider=cfg), grid_spec=grid_spec,
        out_shape=list(out_shape) + [jax.ShapeDtypeStruct((dec_batch, dec_seq, W_MIX), BF16)],
        compiler_params=params, name=name + "_decode_" + rider.mode,
    )(rider.page_table, *args, rider.q, rider.k_new, rider.v_new, rider.cache_k, rider.cache_v)


def _rope_tables(pos):
    inv = ROPE_THETA ** (-jnp.arange(HALF_DIM, dtype=F32) / HALF_DIM)
    ang = pos.astype(F32)[:, None] * inv[None, :]
    cos, sin = jnp.cos(ang), jnp.sin(ang)
    cos_lane = jnp.tile(cos, (1, LANES // HALF_DIM))
    sin_lane = jnp.tile(jnp.concatenate([-sin, sin], axis=1), (1, LANES // HEAD_DIM))
    return cos_lane, sin_lane, cos.T, sin.T


def kernel(x_prompt, x_sample, cache_moba_k, cache_moba_v, cache_sb_k, cache_sb_v, page_table,
           g_mix, w_in, w_branch_moba, w_branch_sb, w_out, g_ffn, w_ffn_gate, w_ffn_up,
           w_ffn_down, g_final):
    batch, seq, d = x_prompt.shape
    dec_batch, dec_seq, _ = x_sample.shape
    depth = w_in.shape[0]
    n_pages = page_table.shape[1]
    past_len = n_pages * PAGE_SIZE
    n_pool = cache_moba_k.shape[1]
    assert seq % MOBA_BLOCK == 0 and past_len % MOBA_BLOCK == 0 and n_pages % PAGES_PER_STEP == 0
    assert (batch * seq) % dec_batch == 0 and seq % ((batch * seq) // dec_batch) == 0
    assert depth == 1

    cos_p, sin_p, cos_pt, sin_pt = _rope_tables(jnp.arange(seq, dtype=jnp.int32))
    cos_p, sin_p = jnp.tile(cos_p, (batch, 1)), jnp.tile(sin_p, (batch, 1))
    cos_s, sin_s, _, _ = _rope_tables(past_len + jnp.arange(dec_seq, dtype=jnp.int32))
    cos_s, sin_s = jnp.tile(cos_s, (dec_batch, 1)), jnp.tile(sin_s, (dec_batch, 1))
    pt_flat = page_table.reshape(-1)

    xp = x_prompt.reshape(batch * seq, d)
    xs = x_sample.reshape(dec_batch * dec_seq, d)
    m_s = dec_batch * dec_seq
    g_last = g_final.reshape(1, d)
    l = 0
    w_in_b = w_in[l].astype(BF16)
    wq = jnp.concatenate([w_in_b[:, 0:W_MIX], w_in_b[:, 3 * W_MIX:4 * W_MIX]], axis=1)
    wkv = jnp.concatenate([w_in_b[:, W_MIX:3 * W_MIX], w_in_b[:, 4 * W_MIX:6 * W_MIX]], axis=1)
    wgt = w_in_b[:, 6 * W_MIX:]
    bf = lambda w: w[l].astype(BF16)
    wa, ws, wo = bf(w_branch_moba), bf(w_branch_sb), bf(w_out)
    wg, wu, wd = bf(w_ffn_gate), bf(w_ffn_up), bf(w_ffn_down)
    gm, gf = g_mix[l].reshape(1, d), g_ffn[l].reshape(1, d)

    ka2, va2, ks2, vs2, qa2, qs2, ga2, gs2 = _proj(xs, gm, wq, wkv, wgt, cos_s, sin_s, bm=m_s)
    r3 = lambda t: t.reshape(dec_batch, dec_seq, W_MIX)
    pool = lambda cch: cch[l].transpose(0, 2, 3, 1).reshape(n_pool, W_MIX, PAGE_SIZE)
    steps_per_seq = 2 * n_pages // PAGES_PER_STEP
    moba_rider = _Rider("moba", pt_flat, r3(qa2), r3(ka2), r3(va2), pool(cache_moba_k),
                        pool(cache_moba_v), n_pages, math.gcd(PROJ_RING_STEPS, steps_per_seq))
    sb_rider = _Rider("sb", pt_flat, r3(qs2), r3(ks2), r3(vs2), pool(cache_sb_k),
                      pool(cache_sb_v), n_pages, math.gcd(POST_RING_STEPS, steps_per_seq))

    bm = (batch * seq) // dec_batch
    ka_t, va_t, ks_t, vs_t, qa, qs, ga, gs, oa2 = _proj(
        xp, gm, wq, wkv.T, wgt, cos_p, sin_p, cos_pt, sin_pt, bm=bm, seq=seq, rider=moba_rider)
    oa, os_ = _mix_prompt(qa, ka_t, va_t, qs, ks_t, vs_t, batch=batch, seq=seq)
    y_prompt, os2 = _post(xp, oa, os_, ga, gs, wa, ws, wo, gf, wg, wu, wd, g_last, bm=bm,
                          rider=sb_rider)

    (y_sample,) = _post(xs, oa2.reshape(m_s, W_MIX), os2.reshape(m_s, W_MIX), ga2, gs2,
                        wa, ws, wo, gf, wg, wu, wd, g_last, bm=m_s)

    rows_p = lambda t: t.reshape(1, batch, N_HEADS, HEAD_DIM, seq).transpose(0, 1, 4, 2, 3)
    rows_s = lambda t: t.reshape(1, dec_batch, dec_seq, N_HEADS, HEAD_DIM)
    return (y_prompt.reshape(batch, seq, d), y_sample.reshape(dec_batch, dec_seq, d),
            rows_p(ka_t), rows_p(va_t), rows_p(ks_t), rows_p(vs_t),
            rows_s(ka2), rows_s(va2), rows_s(ks2), rows_s(vs2))
```

```python
import functools
import itertools
import math
from typing import NamedTuple

import jax
import jax.numpy as jnp
from jax import lax
from jax.experimental import pallas as pl
from jax.experimental.pallas import tpu as pltpu

F32 = jnp.float32
BF16 = jnp.bfloat16

HEAD_DIM = 64
HALF_DIM = HEAD_DIM // 2
N_HEADS = 8
W_MIX = N_HEADS * HEAD_DIM
LANES = 128
N_PAIRS = W_MIX // LANES
MOBA_BLOCK = 256
MOBA_TOPK = 3
Q_BLOCK = 128
PAGE_SIZE = 128
ROPE_THETA = 10000.0
RMS_EPS = 1e-6
NEG_BIG = -1e30
SCORE_SCALE = math.log2(math.e) / math.sqrt(HEAD_DIM)
NT_DIMS = (((1,), (1,)), ((), ()))
VMEM_LIMIT = 60 * 1024 * 1024


def _resident(shape):
    return pl.BlockSpec(shape, lambda *_: (0,) * len(shape), pipeline_mode=pl.Buffered(1))


def _log2_sigmoid(z2):
    return jnp.minimum(z2, 0.0) - jnp.log2(1.0 + jnp.exp2(-jnp.abs(z2)))


def _split_bf16(x):
    hi = x.astype(BF16)
    lo = (x - hi.astype(F32)).astype(BF16)
    return hi, lo


def _rms_norm_bf16(x, g):
    ms = jnp.mean(x * x, axis=-1, keepdims=True)
    return (x * lax.rsqrt(ms + RMS_EPS) * g).astype(BF16)


def _proj_body(*refs, kv_transposed, rider):
    pt_ref, refs = (refs[0], refs[1:]) if rider else (None, refs)
    x_ref, g_ref, wq_ref, wkv_ref, wg_ref, cos_ref, sin_ref = refs[:7]
    refs = refs[7:]
    if kv_transposed:
        cos_t_ref, sin_t_ref = refs[:2]
        refs = refs[2:]
    rider_in, refs = (refs[:5], refs[5:]) if rider else (None, refs)
    ka_ref, va_ref, ks_ref, vs_ref, qa_ref, qs_ref, ga_ref, gs_ref = refs[:8]
    refs = refs[8:]
    rider_out, refs = (refs[:1], refs[1:]) if rider else ((), refs)
    h_s, rider_scratch = refs[0], refs[1:]
    d_model = ga_ref.shape[1]

    h_s[...] = _rms_norm_bf16(x_ref[...], g_ref[...])
    cos = cos_ref[...]
    sin = sin_ref[...]
    lane = lax.broadcasted_iota(jnp.int32, cos.shape, 1)
    first_half = (lane % HEAD_DIM) < HALF_DIM

    def rope_slab(xs):
        rot = jnp.where(first_half, pltpu.roll(xs, LANES - HALF_DIM, 1), pltpu.roll(xs, HALF_DIM, 1))
        return xs * cos + rot * sin

    def queries():
        q = jnp.dot(h_s[...], wq_ref[...], preferred_element_type=F32)
        for p in range(N_PAIRS):
            sl = slice(p * LANES, (p + 1) * LANES)
            qa_ref[:, sl] = (rope_slab(q[:, sl]) * SCORE_SCALE).astype(qa_ref.dtype)
        qs_ref[...] = (q[:, W_MIX:] * SCORE_SCALE).astype(qs_ref.dtype)

    def kv_segment(i):
        if kv_transposed:
            return lax.dot_general(wkv_ref[i * W_MIX:(i + 1) * W_MIX, :], h_s[...], NT_DIMS,
                                   preferred_element_type=F32)
        return jnp.dot(h_s[...], wkv_ref[:, i * W_MIX:(i + 1) * W_MIX], preferred_element_type=F32)

    def moba_keys():
        ka = kv_segment(0)
        if kv_transposed:
            cos_t = cos_t_ref[...]
            sin_t = sin_t_ref[...]
            for hd in range(N_HEADS):
                x1 = ka[hd * HEAD_DIM:hd * HEAD_DIM + HALF_DIM, :]
                x2 = ka[hd * HEAD_DIM + HALF_DIM:(hd + 1) * HEAD_DIM, :]
                ka_ref[hd * HEAD_DIM:hd * HEAD_DIM + HALF_DIM, :] = x1 * cos_t - x2 * sin_t
                ka_ref[hd * HEAD_DIM + HALF_DIM:(hd + 1) * HEAD_DIM, :] = x2 * cos_t + x1 * sin_t
        else:
            for p in range(N_PAIRS):
                sl = slice(p * LANES, (p + 1) * LANES)
                ka_ref[:, sl] = rope_slab(ka[:, sl])

    def plain_segment(i, out_ref):
        def run():
            out_ref[...] = kv_segment(i)
        return run

    def gate(lo, out_ref):
        def run():
            g = jnp.dot(h_s[...], wg_ref[:, lo:lo + d_model], preferred_element_type=F32)
            out_ref[...] = jax.nn.sigmoid(g).astype(out_ref.dtype)
        return run

    work = [queries, moba_keys, plain_segment(1, va_ref), plain_segment(2, ks_ref),
            plain_segment(3, vs_ref), gate(0, ga_ref), gate(d_model, gs_ref)]
    _run_with_rider(work, rider, pt_ref, rider_in, rider_out, rider_scratch)


def _proj(x, g_mix, wq, wkv, wg, cos, sin, cos_t=None, sin_t=None, *, bm, seq=None, rider=None):
    m, d = x.shape
    kv_transposed = seq is not None
    row = lambda width: pl.BlockSpec((bm, width), lambda i, *_: (i, 0))
    in_specs = [row(d), _resident((1, d)), _resident(wq.shape), _resident(wkv.shape),
                _resident(wg.shape), row(LANES), row(LANES)]
    args = [x, g_mix, wq, wkv, wg, cos, sin]
    if kv_transposed:
        per_seq = seq // bm
        tab = pl.BlockSpec((HALF_DIM, bm), lambda i, *_: (0, i % per_seq))
        in_specs += [tab, tab]
        args += [cos_t, sin_t]
        kv_shape = jax.ShapeDtypeStruct((m // seq, W_MIX, seq), F32)
        kv_spec = pl.BlockSpec((None, W_MIX, bm), lambda i, *_: (i // per_seq, 0, i % per_seq))
    else:
        kv_shape = jax.ShapeDtypeStruct((m, W_MIX), F32)
        kv_spec = row(W_MIX)
    out_specs = [kv_spec] * 4 + [row(W_MIX)] * 2 + [row(d)] * 2
    out_shape = [kv_shape] * 4 + [jax.ShapeDtypeStruct((m, W_MIX), BF16)] * 2 \
        + [jax.ShapeDtypeStruct((m, d), BF16)] * 2
    return _call_with_rider(
        functools.partial(_proj_body, kv_transposed=kv_transposed), rider,
        grid=(m // bm,), in_specs=in_specs, args=args, out_specs=out_specs, out_shape=out_shape,
        scratch_shapes=[pltpu.VMEM((bm, d), BF16)],
        name="proj_prompt" if kv_transposed else "proj_decode")


FF_CHUNK = 256


def _post_body(*refs, rider):
    pt_ref, refs = (refs[0], refs[1:]) if rider else (None, refs)
    (x_ref, oa_ref, os_ref, ga_ref, gs_ref, wa_ref, ws_ref, wo_ref,
     gf_ref, wg_ref, wu_ref, wd_ref, gfin_ref) = refs[:13]
    refs = refs[13:]
    rider_in, refs = (refs[:5], refs[5:]) if rider else (None, refs)
    y_ref, refs = refs[0], refs[1:]
    rider_out, refs = (refs[:1], refs[1:]) if rider else ((), refs)
    h_s, acc_s, rider_scratch = refs[0], refs[1], refs[2:]
    d_ff = wg_ref.shape[1]

    def merge_branches():
        merged = (ga_ref[...].astype(F32) * jnp.dot(oa_ref[...], wa_ref[...], preferred_element_type=F32)
                  + gs_ref[...].astype(F32) * jnp.dot(os_ref[...], ws_ref[...], preferred_element_type=F32))
        h_s[...] = merged.astype(BF16)

    def out_projection():
        x1 = x_ref[...] + jnp.dot(h_s[...], wo_ref[...], preferred_element_type=F32)
        acc_s[...] = x1
        h_s[...] = _rms_norm_bf16(x1, gf_ref[...])

    def ffn_columns(lo):
        def run():
            h2 = h_s[...]
            gate = jnp.dot(h2, wg_ref[:, lo:lo + FF_CHUNK], preferred_element_type=F32)
            up = jnp.dot(h2, wu_ref[:, lo:lo + FF_CHUNK], preferred_element_type=F32)
            ff = (gate * jax.nn.sigmoid(gate) * up).astype(BF16)
            acc_s[...] += jnp.dot(ff, wd_ref[lo:lo + FF_CHUNK, :], preferred_element_type=F32)
        return run

    def closing_norm():
        x2 = acc_s[...]
        ms2 = jnp.mean(x2 * x2, axis=-1, keepdims=True)
        y_ref[...] = x2 * lax.rsqrt(ms2 + RMS_EPS) * gfin_ref[...]

    work = [merge_branches, out_projection] + [ffn_columns(lo) for lo in range(0, d_ff, FF_CHUNK)] \
        + [closing_norm]
    _run_with_rider(work, rider, pt_ref, rider_in, rider_out, rider_scratch)


def _post(x, oa, os_, ga, gs, wa, ws, wo, g_ffn, wg, wu, wd, g_final, *, bm, rider=None):
    m, d = x.shape
    d_ff = wg.shape[1]
    assert d_ff % FF_CHUNK == 0
    row = lambda width: pl.BlockSpec((bm, width), lambda i, *_: (i, 0))
    return _call_with_rider(
        _post_body, rider, grid=(m // bm,),
        in_specs=[row(d), row(W_MIX), row(W_MIX), row(d), row(d),
                  _resident((W_MIX, d)), _resident((W_MIX, d)), _resident((d, d)),
                  _resident((1, d)), _resident((d, d_ff)), _resident((d, d_ff)),
                  _resident((d_ff, d)), _resident((1, d))],
        args=[x, oa, os_, ga, gs, wa, ws, wo, g_ffn, wg, wu, wd, g_final],
        out_specs=[row(d)], out_shape=[jax.ShapeDtypeStruct((m, d), F32)],
        scratch_shapes=[pltpu.VMEM((bm, d), BF16), pltpu.VMEM((bm, d), F32)],
        name="post_ffn")


GATE_GROUP = 8
HEADS_PER_STAGE = 8


def _moba_parts(c, q_ref, k_ref, v_ref, o_ref, kaug, vb, kmhi, kmlo, lhs, m_s, acc_s, *, seq):
    nb = seq // MOBA_BLOCK
    assert nb <= GATE_GROUP and N_HEADS * GATE_GROUP <= LANES

    @pl.when(c == 0)
    def _per_sequence_setup():
        r = lax.broadcasted_iota(jnp.int32, (LANES, MOBA_BLOCK), 0)
        lane_w = lax.broadcasted_iota(jnp.int32, (W_MIX, LANES), 1)
        row_w = lax.broadcasted_iota(jnp.int32, (W_MIX, LANES), 0)
        km = jnp.zeros((W_MIX, LANES), F32)
        for t in range(nb):
            sl = slice(t * MOBA_BLOCK, (t + 1) * MOBA_BLOCK)
            ind = jnp.where((r < N_HEADS * GATE_GROUP) & ((r % GATE_GROUP) == t), 1.0, 0.0).astype(BF16)
            for p in range(N_PAIRS):
                kaug[p, t, 0:LANES, :] = k_ref[p * LANES:(p + 1) * LANES, sl].astype(BF16)
                kaug[p, t, LANES:2 * LANES, :] = ind
                vb[p, t, 0:LANES, :] = v_ref[p * LANES:(p + 1) * LANES, sl].astype(BF16)
                vb[p, t, LANES:2 * LANES, :] = jnp.ones((LANES, MOBA_BLOCK), BF16)
            block_sum = jnp.sum(k_ref[:, sl], axis=1, keepdims=True)
            km = jnp.where((lane_w % GATE_GROUP) == t, block_sum, km)
        km = jnp.where((lane_w < N_HEADS * GATE_GROUP) & ((row_w // HEAD_DIM) == (lane_w // GATE_GROUP)),
                       km * (1.0 / MOBA_BLOCK), 0.0)
        hi, lo = _split_bf16(km)
        kmhi[...] = hi
        kmlo[...] = lo

    q = q_ref[...]
    gate = (jnp.dot(q, kmhi[...], preferred_element_type=F32)
            + jnp.dot(q, kmlo[...], preferred_element_type=F32))
    lane = lax.broadcasted_iota(jnp.int32, (Q_BLOCK, LANES), 1)
    n = lane % GATE_GROUP
    cur = (c * Q_BLOCK) // MOBA_BLOCK
    g = jnp.where(n < cur, gate, -jnp.inf)
    rank = jnp.zeros((Q_BLOCK, LANES), F32)
    for r in range(1, GATE_GROUP):
        wraps = (n + r) >= GATE_GROUP
        other = jnp.where(wraps, pltpu.roll(g, GATE_GROUP - r, 1), pltpu.roll(g, LANES - r, 1))
        beats = (other > g) | (wraps & (other == g))
        rank = rank + beats.astype(F32)
    keep = ((n < cur) & (rank < MOBA_TOPK)) | (n >= cur)
    bias = jnp.where(keep | (lane >= N_HEADS * GATE_GROUP), 0.0, NEG_BIG)

    qf = q.astype(F32)
    for h in range(N_HEADS):
        p, j = divmod(h, 2)
        in_head = (lane >= j * HEAD_DIM) & (lane < (j + 1) * HEAD_DIM)
        lhs[h, :, 0:LANES] = jnp.where(in_head, qf[:, p * LANES:(p + 1) * LANES], 0.0).astype(BF16)
        in_group = (lane >= h * GATE_GROUP) & (lane < (h + 1) * GATE_GROUP)
        lhs[h, :, LANES:2 * LANES] = jnp.where(in_group, bias, 0.0).astype(BF16)
    m_s[...] = jnp.full(m_s.shape, NEG_BIG, F32)
    acc_s[...] = jnp.zeros(acc_s.shape, F32)

    qq = lax.broadcasted_iota(jnp.int32, (Q_BLOCK, MOBA_BLOCK), 0)
    kk = lax.broadcasted_iota(jnp.int32, (Q_BLOCK, MOBA_BLOCK), 1)
    causal = kk <= qq + (c * Q_BLOCK - cur * MOBA_BLOCK)
    twice = lambda x: jnp.concatenate([x, x], axis=1)

    def tile_group(t0, n_tiles, own_last):
        tiles = range(n_tiles)
        for h0 in range(0, N_HEADS, HEADS_PER_STAGE):
            heads = range(h0, h0 + HEADS_PER_STAGE)
            s = {h: [jnp.dot(lhs[h], kaug[h // 2, t0 + i], preferred_element_type=F32)
                     for i in tiles] for h in heads}
            yield
            if own_last:
                for h in heads:
                    s[h][-1] = jnp.where(causal, s[h][-1], NEG_BIG)
            m_old = {h: m_s[h] for h in heads}
            m_new = {}
            for h in heads:
                top = functools.reduce(jnp.maximum, s[h])
                m_new[h] = jnp.maximum(m_old[h], jnp.max(top, axis=1, keepdims=True))
            alpha = {h: jnp.exp2(m_old[h] - m_new[h]) for h in heads}
            yield
            pr = {h: [jnp.exp2(s[h][i] - twice(m_new[h])).astype(BF16) for i in tiles]
                  for h in heads}
            yield
            for h in heads:
                pv = sum(lax.dot_general(pr[h][i], vb[h // 2, t0 + i], NT_DIMS,
                                         preferred_element_type=F32) for i in tiles)
                acc_s[h] = twice(alpha[h]) * acc_s[h] + pv
                m_s[h] = m_new[h]
            yield

    def finish():
        for p in range(N_PAIRS):
            acc_a, acc_b = acc_s[2 * p], acc_s[2 * p + 1]
            oa = acc_a[:, :LANES] / acc_a[:, LANES:]
            ob = acc_b[:, :LANES] / acc_b[:, LANES:]
            o_ref[:, p * LANES:(p + 1) * LANES] = jnp.where(lane < HEAD_DIM, oa, ob).astype(o_ref.dtype)

    return tile_group, finish, cur


def _moba_scratch(seq):
    nb = seq // MOBA_BLOCK
    return [pltpu.VMEM((N_PAIRS, nb, 2 * LANES, MOBA_BLOCK), BF16),
            pltpu.VMEM((N_PAIRS, nb, 2 * LANES, MOBA_BLOCK), BF16),
            pltpu.VMEM((W_MIX, LANES), BF16),
            pltpu.VMEM((W_MIX, LANES), BF16),
            pltpu.VMEM((N_HEADS, Q_BLOCK, 2 * LANES), BF16),
            pltpu.VMEM((N_HEADS, Q_BLOCK, LANES), F32),
            pltpu.VMEM((N_HEADS, Q_BLOCK, 2 * LANES), F32)]


SB_TILE = 256


def _suffix_sum_matrix(n):
    r = lax.broadcasted_iota(jnp.int32, (n, n), 0)
    c = lax.broadcasted_iota(jnp.int32, (n, n), 1)
    return jnp.where(r > c, 1.0, 0.0).astype(BF16)


def _sb_log_terms(z2, past, tri):
    ls = _log2_sigmoid(z2)
    lk = ls - z2
    if past is not None:
        lk = jnp.where(past, lk, 0.0)
    tail = jnp.dot(lk.astype(BF16), tri, preferred_element_type=F32)
    return ls, lk, tail


def _sb_weights(z2, past, later, tri):
    ls, lk, tail = _sb_log_terms(z2, past, tri)
    a = jnp.exp2(ls + tail + later)
    if past is not None:
        a = jnp.where(past, a, 0.0)
    return a, jnp.sum(lk, axis=1, keepdims=True)


def _sb_parts(c, q_ref, k_ref, v_ref, o_ref, kb, vb, tri, later_s, acc_s, *, seq):
    @pl.when(c == 0)
    def _per_sequence_setup():
        for t in range(seq // SB_TILE):
            sl = slice(t * SB_TILE, (t + 1) * SB_TILE)
            for p in range(N_PAIRS):
                kb[p, t] = k_ref[p * LANES:(p + 1) * LANES, sl].astype(BF16)
                vb[p, t] = v_ref[p * LANES:(p + 1) * LANES, sl].astype(BF16)
        tri[...] = _suffix_sum_matrix(SB_TILE)

    qf = q_ref[...].astype(F32)
    lane = lax.broadcasted_iota(jnp.int32, (Q_BLOCK, LANES), 1)
    qm = []
    for h in range(N_HEADS):
        p, j = divmod(h, 2)
        in_head = (lane >= j * HEAD_DIM) & (lane < (j + 1) * HEAD_DIM)
        qm.append(jnp.where(in_head, qf[:, p * LANES:(p + 1) * LANES], 0.0).astype(BF16))
    later_s[...] = jnp.zeros(later_s.shape, F32)
    acc_s[...] = jnp.zeros(acc_s.shape, F32)

    diag = (c * Q_BLOCK) // SB_TILE
    qq = lax.broadcasted_iota(jnp.int32, (Q_BLOCK, SB_TILE), 0)
    kk = lax.broadcasted_iota(jnp.int32, (Q_BLOCK, SB_TILE), 1)
    past_diag = kk < qq + (c * Q_BLOCK - diag * SB_TILE)

    def tile_group(t0, n_tiles, diag_last):
        tiles = range(n_tiles)
        masked = lambda i: diag_last and i == n_tiles - 1
        u = tri[...]
        for h0 in range(0, N_HEADS, HEADS_PER_STAGE):
            heads = range(h0, h0 + HEADS_PER_STAGE)
            z = {h: [jnp.dot(qm[h], kb[h // 2, t0 + i], preferred_element_type=F32) for i in tiles]
                 for h in heads}
            yield
            ls = {h: [_log2_sigmoid(z[h][i]) for i in tiles] for h in heads}
            lk = {h: [ls[h][i] - z[h][i] for i in tiles] for h in heads}
            lk = {h: [jnp.where(past_diag, lk[h][i], 0.0) if masked(i) else lk[h][i]
                      for i in tiles] for h in heads}
            yield
            tail = {h: [jnp.dot(lk[h][i].astype(BF16), u, preferred_element_type=F32)
                        for i in tiles] for h in heads}
            yield
            a = {h: [jnp.exp2(ls[h][i] + tail[h][i]) for i in tiles] for h in heads}
            a = {h: [jnp.where(past_diag, a[h][i], 0.0) if masked(i) else a[h][i] for i in tiles]
                 for h in heads}
            yield
            for h in heads:
                later = later_s[h]
                out = acc_s[h]
                for i in reversed(tiles):
                    pv = lax.dot_general(a[h][i].astype(BF16), vb[h // 2, t0 + i], NT_DIMS,
                                         preferred_element_type=F32)
                    out = out + jnp.exp2(later) * pv
                    later = later + jnp.sum(lk[h][i], axis=1, keepdims=True)
                acc_s[h] = out
                later_s[h] = later
            yield

    def finish():
        for p in range(N_PAIRS):
            o_ref[:, p * LANES:(p + 1) * LANES] = jnp.where(
                lane < HEAD_DIM, acc_s[2 * p], acc_s[2 * p + 1]).astype(o_ref.dtype)

    return tile_group, finish, diag


def _sb_scratch(seq):
    nt = seq // SB_TILE
    return [pltpu.VMEM((N_PAIRS, nt, LANES, SB_TILE), BF16),
            pltpu.VMEM((N_PAIRS, nt, LANES, SB_TILE), BF16),
            pltpu.VMEM((SB_TILE, SB_TILE), BF16),
            pltpu.VMEM((N_HEADS, Q_BLOCK, LANES), F32),
            pltpu.VMEM((N_HEADS, Q_BLOCK, LANES), F32)]


def _mix_prompt_body(qa_ref, ka_ref, va_ref, qs_ref, ks_ref, vs_ref, oa_ref, os_ref, *scratch, seq):
    c = pl.program_id(1)
    n_moba = len(_moba_scratch(seq))
    moba_group, moba_finish, own = _moba_parts(c, qa_ref, ka_ref, va_ref, oa_ref,
                                               *scratch[:n_moba], seq=seq)
    sb_group, sb_finish, diag = _sb_parts(c, qs_ref, ks_ref, vs_ref, os_ref,
                                          *scratch[n_moba:], seq=seq)
    assert MOBA_BLOCK == SB_TILE

    def together(*generators):
        for _ in itertools.zip_longest(*generators):
            pass

    @pl.when(own % 2 == 0)
    def _last_tile_alone():
        together(moba_group(own, 1, True), sb_group(diag, 1, True))

    @pl.when(own % 2 == 1)
    def _last_two_tiles():
        together(moba_group(own - 1, 2, True), sb_group(diag - 1, 2, True))

    n_pairs = own // 2

    def earlier_pair(i, carry):
        together(moba_group(2 * i, 2, False), sb_group(2 * (n_pairs - 1 - i), 2, False))
        return carry

    lax.fori_loop(0, n_pairs, earlier_pair, 0)
    moba_finish()
    sb_finish()


def _mix_prompt(qa, ka_t, va_t, qs, ks_t, vs_t, *, batch, seq):
    nq = seq // Q_BLOCK
    q_spec = pl.BlockSpec((Q_BLOCK, W_MIX), lambda b, c: (b * nq + c, 0))
    kv_spec = pl.BlockSpec((None, W_MIX, seq), lambda b, c: (b, 0, 0))
    out = jax.ShapeDtypeStruct((batch * seq, W_MIX), BF16)
    return pl.pallas_call(
        functools.partial(_mix_prompt_body, seq=seq),
        grid=(batch, nq),
        in_specs=[q_spec, kv_spec, kv_spec, q_spec, kv_spec, kv_spec],
        out_specs=[q_spec, q_spec],
        out_shape=[out, out],
        scratch_shapes=_moba_scratch(seq) + _sb_scratch(seq),
        compiler_params=pltpu.CompilerParams(
            dimension_semantics=("arbitrary", "arbitrary"), vmem_limit_bytes=VMEM_LIMIT),
        name="mix_prompt",
    )(qa, ka_t, va_t, qs, ks_t, vs_t)


PAGES_PER_STEP = 8
STEP_KEYS = PAGES_PER_STEP * PAGE_SIZE
PROJ_RING_STEPS = 8
POST_RING_STEPS = 8


class _Rider(NamedTuple):
    mode: str
    page_table: jax.Array
    q: jax.Array
    k_new: jax.Array
    v_new: jax.Array
    cache_k: jax.Array
    cache_v: jax.Array
    n_pages: int
    ring_steps: int


class _RiderConfig(NamedTuple):
    mode: str
    n_pages: int
    dec_seq: int
    n_batch: int
    ring_steps: int


def _decode_scratch(cfg):
    npg = PAGES_PER_STEP
    n_groups = cfg.n_pages // npg
    rows = N_HEADS * cfg.dec_seq
    return [pltpu.VMEM((cfg.ring_steps * npg, W_MIX, PAGE_SIZE), F32),
            pltpu.SemaphoreType.DMA((cfg.ring_steps,)),
            pltpu.VMEM((rows, W_MIX), BF16),
            pltpu.VMEM((n_groups, rows, STEP_KEYS), F32),
            pltpu.VMEM((n_groups, rows, STEP_KEYS), BF16),
            pltpu.VMEM((rows, LANES), BF16),
            pltpu.VMEM((rows, 1), F32),
            pltpu.VMEM((rows, W_MIX), F32)]


def _decode_steps(cfg, b, pt_ref, q_ref, kn_ref, vn_ref, ck_hbm, cv_hbm, o_ref, scratch):
    pages, sems, qbd, z_s, a_s, anew_s, norm_s, o_acc = scratch
    mode, n_pages, dec_seq, n_batch, ring_steps = cfg
    npg = PAGES_PER_STEP
    n_groups = n_pages // npg
    n_steps = 2 * n_groups
    lookahead = ring_steps - 1
    assert n_steps % ring_steps == 0
    rows = N_HEADS * dec_seq
    assert rows <= LANES and dec_seq <= 8

    def page_copy(step, j, batch):
        src = ck_hbm if step < n_groups else cv_hbm
        page = (step % n_groups) * npg + j
        ring = step % ring_steps
        return pltpu.make_async_copy(src.at[pt_ref[batch * n_pages + page]],
                                     pages.at[ring * npg + j], sems.at[ring])

    def start_step(step, batch):
        for j in range(npg):
            page_copy(step, j, batch).start()

    def wait_step(step):
        for j in range(npg):
            page_copy(step, j, b).wait()

    def step_pages(step):
        ring = step % ring_steps
        return jnp.concatenate([pages[ring * npg + j].astype(BF16) for j in range(npg)], axis=1)

    row_i = lax.broadcasted_iota(jnp.int32, (rows, LANES), 0) % dec_seq
    lane = lax.broadcasted_iota(jnp.int32, (rows, LANES), 1)

    def prime():
        @pl.when(b == 0)
        def _prime_ring():
            for s in range(lookahead):
                start_step(s, 0)

    def block_diagonal_queries():
        qf = q_ref[...].astype(F32)
        qt = jnp.concatenate([qf] * N_HEADS, axis=0)
        r = lax.broadcasted_iota(jnp.int32, qt.shape, 0)
        l = lax.broadcasted_iota(jnp.int32, qt.shape, 1)
        qbd[...] = jnp.where((l // HEAD_DIM) == (r // dec_seq), qt, 0.0).astype(BF16)

    def weights():
        kn = jnp.concatenate([kn_ref[...], jnp.zeros((LANES - dec_seq, W_MIX), F32)], axis=0)
        zn = lax.dot_general(qbd[...], kn.astype(BF16), NT_DIMS, preferred_element_type=F32)
        if mode == "sb":
            tri = _suffix_sum_matrix(SB_TILE)
            past_new = lane < row_i
            a_new, later = _sb_weights(zn, past_new, 0.0, _suffix_sum_matrix(LANES))
            anew_s[...] = a_new.astype(BF16)
            for gg in range(n_groups - 1, -1, -1):
                for t in range(STEP_KEYS // SB_TILE - 1, -1, -1):
                    sl = slice(t * SB_TILE, (t + 1) * SB_TILE)
                    a, tot = _sb_weights(z_s[gg, :, sl], None, later, tri)
                    a_s[gg, :, sl] = a.astype(BF16)
                    later = later + tot
            norm_s[...] = jnp.ones(norm_s.shape, F32)
        else:
            blocks_per_step = STEP_KEYS // MOBA_BLOCK
            n_blocks = n_groups * blocks_per_step
            assert n_blocks <= LANES
            gate = jnp.zeros((rows, LANES), F32)
            for gg in range(n_groups):
                for t in range(blocks_per_step):
                    blk = z_s[gg, :, t * MOBA_BLOCK:(t + 1) * MOBA_BLOCK]
                    gate = jnp.where(lane == gg * blocks_per_step + t,
                                     jnp.sum(blk, axis=1, keepdims=True), gate)
            gm = jnp.where(lane < n_blocks, gate, -jnp.inf)
            sel = jnp.zeros((rows, LANES), F32)
            for _ in range(min(MOBA_TOPK, n_blocks)):
                mx = jnp.max(gm, axis=1, keepdims=True)
                idx = jnp.min(jnp.where(gm == mx, lane, LANES), axis=1, keepdims=True)
                pick = lane == idx
                sel = jnp.where(pick, 1.0, sel)
                gm = jnp.where(pick, -jnp.inf, gm)
            sel_b = sel.astype(BF16)
            own = (lane <= row_i) & (lane < dec_seq)
            zn = jnp.where(own, zn, NEG_BIG)
            mx = jnp.max(zn, axis=1, keepdims=True)
            masked = []
            for gg in range(n_groups):
                bl = lax.broadcasted_iota(jnp.int32, (LANES, STEP_KEYS), 0)
                bk = lax.broadcasted_iota(jnp.int32, (LANES, STEP_KEYS), 1)
                expand = jnp.where(bl == gg * blocks_per_step + bk // MOBA_BLOCK, 1.0, 0.0).astype(BF16)
                chosen = jnp.dot(sel_b, expand, preferred_element_type=F32) > 0.5
                s = jnp.where(chosen, z_s[gg], NEG_BIG)
                masked.append(s)
                mx = jnp.maximum(mx, jnp.max(s, axis=1, keepdims=True))
            p_new = jnp.exp2(zn - mx)
            denom = jnp.sum(p_new, axis=1, keepdims=True)
            anew_s[...] = p_new.astype(BF16)
            for gg in range(n_groups):
                pr = jnp.exp2(masked[gg] - mx)
                denom = denom + jnp.sum(pr, axis=1, keepdims=True)
                a_s[gg] = pr.astype(BF16)
            norm_s[...] = denom

    def write_out():
        o = o_acc[...] / norm_s[...]
        hd = lax.broadcasted_iota(jnp.int32, (dec_seq, W_MIX), 1) // HEAD_DIM
        out = jnp.zeros((dec_seq, W_MIX), F32)
        for h in range(N_HEADS):
            out = jnp.where(hd == h, o[h * dec_seq:(h + 1) * dec_seq, :], out)
        o_ref[...] = out.astype(o_ref.dtype)

    def make_step(step):
        def run():
            ahead = step + lookahead
            if ahead < n_steps:
                start_step(ahead, b)
            else:
                @pl.when(b + 1 < n_batch)
                def _start_next_sequence():
                    start_step(ahead - n_steps, b + 1)
            wait_step(step)
            if step == 0:
                block_diagonal_queries()
            if step < n_groups:
                z_s[step] = jnp.dot(qbd[...], step_pages(step), preferred_element_type=F32)
                if step == n_groups - 1:
                    weights()
                    vn = jnp.concatenate([vn_ref[...], jnp.zeros((LANES - dec_seq, W_MIX), F32)], axis=0)
                    o_acc[...] = jnp.dot(anew_s[...], vn.astype(BF16), preferred_element_type=F32)
            else:
                o_acc[...] = o_acc[...] + lax.dot_general(
                    a_s[step - n_groups], step_pages(step), NT_DIMS, preferred_element_type=F32)
            if step == n_steps - 1:
                write_out()
        return run

    return prime, [make_step(step) for step in range(n_steps)]


def _run_with_rider(work, cfg, pt_ref, rider_in, rider_out, rider_scratch):
    if cfg is None:
        for item in work:
            item()
        return
    prime, steps = _decode_steps(cfg, pl.program_id(0), pt_ref, *rider_in, *rider_out, rider_scratch)
    prime()
    done = 0
    for i, item in enumerate(work):
        item()
        upto = len(steps) * (i + 1) // len(work)
        for step in steps[done:upto]:
            step()
        done = upto


def _call_with_rider(body, rider, *, grid, in_specs, args, out_specs, out_shape, scratch_shapes, name):
    params = pltpu.CompilerParams(dimension_semantics=("arbitrary",) * len(grid),
                                  vmem_limit_bytes=VMEM_LIMIT)
    if rider is None:
        return pl.pallas_call(
            functools.partial(body, rider=None), grid=grid, in_specs=in_specs, out_specs=out_specs,
            out_shape=out_shape, scratch_shapes=scratch_shapes, compiler_params=params, name=name,
        )(*args)
    dec_batch, dec_seq, _ = rider.q.shape
    assert grid == (dec_batch,), "a rider needs one host grid step per decode sequence"
    cfg = _RiderConfig(rider.mode, rider.n_pages, dec_seq, dec_batch, rider.ring_steps)
    small = pl.BlockSpec((None, dec_seq, W_MIX), lambda i, *_: (i, 0, 0))
    hbm = pl.BlockSpec(memory_space=pl.ANY)
    grid_spec = pltpu.PrefetchScalarGridSpec(
        num_scalar_prefetch=1, grid=grid,
        in_specs=list(in_specs) + [small, small, small, hbm, hbm],
        out_specs=list(out_specs) + [small],
        scratch_shapes=list(scratch_shapes) + _decode_scratch(cfg))
    return pl.pallas_call(
        functools.partial(body, rider=cfg), grid_spec=grid_spec,
        out_shape=list(out_shape) + [jax.ShapeDtypeStruct((dec_batch, dec_seq, W_MIX), BF16)],
        compiler_params=params, name=name + "_decode_" + rider.mode,
    )(rider.page_table, *args, rider.q, rider.k_new, rider.v_new, rider.cache_k, rider.cache_v)


def _rope_tables(pos):
    inv = ROPE_THETA ** (-jnp.arange(HALF_DIM, dtype=F32) / HALF_DIM)
    ang = pos.astype(F32)[:, None] * inv[None, :]
    cos, sin = jnp.cos(ang), jnp.sin(ang)
    cos_lane = jnp.tile(cos, (1, LANES // HALF_DIM))
    sin_lane = jnp.tile(jnp.concatenate([-sin, sin], axis=1), (1, LANES // HEAD_DIM))
    return cos_lane, sin_lane, cos.T, sin.T


def kernel(x_prompt, x_sample, cache_moba_k, cache_moba_v, cache_sb_k, cache_sb_v, page_table,
           g_mix, w_in, w_branch_moba, w_branch_sb, w_out, g_ffn, w_ffn_gate, w_ffn_up,
           w_ffn_down, g_final):
    batch, seq, d = x_prompt.shape
    dec_batch, dec_seq, _ = x_sample.shape
    depth = w_in.shape[0]
    n_pages = page_table.shape[1]
    past_len = n_pages * PAGE_SIZE
    n_pool = cache_moba_k.shape[1]
    assert seq % MOBA_BLOCK == 0 and past_len % MOBA_BLOCK == 0 and n_pages % PAGES_PER_STEP == 0
    assert (batch * seq) % dec_batch == 0 and seq % ((batch * seq) // dec_batch) == 0
    assert depth == 1

    cos_p, sin_p, cos_pt, sin_pt = _rope_tables(jnp.arange(seq, dtype=jnp.int32))
    cos_p, sin_p = jnp.tile(cos_p, (batch, 1)), jnp.tile(sin_p, (batch, 1))
    cos_s, sin_s, _, _ = _rope_tables(past_len + jnp.arange(dec_seq, dtype=jnp.int32))
    cos_s, sin_s = jnp.tile(cos_s, (dec_batch, 1)), jnp.tile(sin_s, (dec_batch, 1))
    pt_flat = page_table.reshape(-1)

    xp = x_prompt.reshape(batch * seq, d)
    xs = x_sample.reshape(dec_batch * dec_seq, d)
    m_s = dec_batch * dec_seq
    g_last = g_final.reshape(1, d)
    l = 0
    w_in_b = w_in[l].astype(BF16)
    wq = jnp.concatenate([w_in_b[:, 0:W_MIX], w_in_b[:, 3 * W_MIX:4 * W_MIX]], axis=1)
    wkv = jnp.concatenate([w_in_b[:, W_MIX:3 * W_MIX], w_in_b[:, 4 * W_MIX:6 * W_MIX]], axis=1)
    wgt = w_in_b[:, 6 * W_MIX:]
    bf = lambda w: w[l].astype(BF16)
    wa, ws, wo = bf(w_branch_moba), bf(w_branch_sb), bf(w_out)
    wg, wu, wd = bf(w_ffn_gate), bf(w_ffn_up), bf(w_ffn_down)
    gm, gf = g_mix[l].reshape(1, d), g_ffn[l].reshape(1, d)

    ka2, va2, ks2, vs2, qa2, qs2, ga2, gs2 = _proj(xs, gm, wq, wkv, wgt, cos_s, sin_s, bm=m_s)
    r3 = lambda t: t.reshape(dec_batch, dec_seq, W_MIX)
    pool = lambda cch: cch[l].transpose(0, 2, 3, 1).reshape(n_pool, W_MIX, PAGE_SIZE)
    steps_per_seq = 2 * n_pages // PAGES_PER_STEP
    moba_rider = _Rider("moba", pt_flat, r3(qa2), r3(ka2), r3(va2), pool(cache_moba_k),
                        pool(cache_moba_v), n_pages, math.gcd(PROJ_RING_STEPS, steps_per_seq))
    sb_rider = _Rider("sb", pt_flat, r3(qs2), r3(ks2), r3(vs2), pool(cache_sb_k),
                      pool(cache_sb_v), n_pages, math.gcd(POST_RING_STEPS, steps_per_seq))

    bm = (batch * seq) // dec_batch
    ka_t, va_t, ks_t, vs_t, qa, qs, ga, gs, oa2 = _proj(
        xp, gm, wq, wkv.T, wgt, cos_p, sin_p, cos_pt, sin_pt, bm=bm, seq=seq, rider=moba_rider)
    oa, os_ = _mix_prompt(qa, ka_t, va_t, qs, ks_t, vs_t, batch=batch, seq=seq)
    y_prompt, os2 = _post(xp, oa, os_, ga, gs, wa, ws, wo, gf, wg, wu, wd, g_last, bm=bm,
                          rider=sb_rider)

    (y_sample,) = _post(xs, oa2.reshape(m_s, W_MIX), os2.reshape(m_s, W_MIX), ga2, gs2,
                        wa, ws, wo, gf, wg, wu, wd, g_last, bm=m_s)

    rows_p = lambda t: t.reshape(1, batch, N_HEADS, HEAD_DIM, seq).transpose(0, 1, 4, 2, 3)
    rows_s = lambda t: t.reshape(1, dec_batch, dec_seq, N_HEADS, HEAD_DIM)
    return (y_prompt.reshape(batch, seq, d), y_sample.reshape(dec_batch, dec_seq, d),
            rows_p(ka_t), rows_p(va_t), rows_p(ks_t), rows_p(vs_t),
            rows_s(ka2), rows_s(va2), rows_s(ks2), rows_s(vs2))
```
